```python
import jax, jax.numpy as jnp
from jax import lax
import numpy as np

D_MODEL = 1024
BATCH = 2
SEQ = 8192
DEPTH = 2

PLE_DIM = 256
D_CONF = D_MODEL // 2
D_SC = D_MODEL // 2
N_GROUPS_CONF = 8
N_GROUPS_SC = 8
CONF_KERNEL = 31
SC_KERNEL = 3
FFN_KERNEL = 3
D_FF = 2816
EPS = 1e-6

W_IN_COLS = 2 * D_CONF + 3 * D_SC + 2 * D_MODEL

kernel_name = "hybrid_conformer_shortconv_gated_merge"


def rmsnorm(x, g):
    xf = x.astype(jnp.float32)
    y = xf * lax.rsqrt(jnp.mean(xf * xf, axis=-1, keepdims=True) + EPS)
    return (y * g.astype(jnp.float32)).astype(x.dtype)


def layernorm(x, g, b):
    xf = x.astype(jnp.float32)
    mu = jnp.mean(xf, axis=-1, keepdims=True)
    var = jnp.mean(jnp.square(xf - mu), axis=-1, keepdims=True)
    y = (xf - mu) * lax.rsqrt(var + EPS)
    return (y * g.astype(jnp.float32) + b.astype(jnp.float32)).astype(x.dtype)


def causal_dwconv(u, w):
    k, c = w.shape
    return lax.conv_general_dilated(
        u, w[:, None, :].astype(u.dtype),
        window_strides=(1,), padding=[(k - 1, 0)],
        dimension_numbers=("NWC", "WIO", "NWC"),
        feature_group_count=c)


def setup_inputs(seed: int = 0) -> dict:
    key = jax.random.key(seed)
    ks = jax.random.split(key, 24)
    f32 = jnp.float32

    def dense(k, shape, fan_in):
        return jax.random.normal(k, shape, f32) * (fan_in ** -0.5)

    def gain(k, shape):
        return 1.0 + 0.05 * jax.random.normal(k, shape, f32)

    def small(k, shape):
        return 0.02 * jax.random.normal(k, shape, f32)

    L = DEPTH
    return {
        "x": jax.random.normal(ks[0], (BATCH, SEQ, D_MODEL), f32),
        "p": jax.random.normal(ks[1], (DEPTH, BATCH, SEQ, PLE_DIM), f32),
        "g_mix": gain(ks[2], (L, D_MODEL)),
        "w_in": dense(ks[3], (L, D_MODEL, W_IN_COLS), D_MODEL),
        "b_gate": small(ks[4], (L, 2 * D_MODEL)),
        "conv_a_w": dense(ks[5], (L, CONF_KERNEL, D_CONF), CONF_KERNEL),
        "conv_a_b": small(ks[6], (L, D_CONF)),
        "ln_a_g": gain(ks[7], (L, D_CONF)),
        "ln_a_b": small(ks[8], (L, D_CONF)),
        "w_a_out": dense(ks[9], (L, D_CONF, D_MODEL), D_CONF),
        "conv_b_w": dense(ks[10], (L, SC_KERNEL, D_SC), SC_KERNEL),
        "w_b_out": dense(ks[11], (L, D_SC, D_MODEL), D_SC),
        "w_o": dense(ks[12], (L, D_MODEL, D_MODEL), D_MODEL),
        "g_ffn": gain(ks[13], (L, D_MODEL)),
        "w_up": dense(ks[14], (L, D_MODEL, 2 * D_FF), D_MODEL),
        "conv_f_w": dense(ks[15], (L, FFN_KERNEL, D_FF), FFN_KERNEL),
        "conv_f_b": small(ks[16], (L, D_FF)),
        "w_down": dense(ks[17], (L, D_FF, D_MODEL), D_FF),
        "g_ple": gain(ks[18], (L, D_MODEL)),
        "w_ple": dense(ks[19], (L, PLE_DIM, D_MODEL), PLE_DIM),
        "w_ple_gate": dense(ks[20], (L, D_MODEL, D_MODEL), D_MODEL),
        "g_final": gain(ks[21], (D_MODEL,)),
    }


def reference(x, p, g_mix, w_in, b_gate, conv_a_w, conv_a_b, ln_a_g, ln_a_b, w_a_out,
              conv_b_w, w_b_out, w_o, g_ffn, w_up, conv_f_w, conv_f_b, w_down,
              g_ple, w_ple, w_ple_gate, g_final):
    o1 = D_CONF
    o2 = o1 + D_CONF
    o3 = o2 + D_SC
    o4 = o3 + D_SC
    o5 = o4 + D_SC
    o6 = o5 + D_MODEL
    for i in range(DEPTH):
        h = rmsnorm(x, g_mix[i])
        z = jnp.einsum("bsd,dn->bsn", h, w_in[i])
        a_val, a_gt = z[..., :o1], z[..., o1:o2]
        sc_b, sc_c, sc_v = z[..., o2:o3], z[..., o3:o4], z[..., o4:o5]
        gate_logits = z[..., o5:] + b_gate[i]
        g_a = jax.nn.sigmoid(gate_logits[..., :D_MODEL])
        g_b = jax.nn.sigmoid(gate_logits[..., D_MODEL:])

        a = a_val * jax.nn.sigmoid(a_gt)
        a = causal_dwconv(a, conv_a_w[i]) + conv_a_b[i]
        a = jax.nn.silu(layernorm(a, ln_a_g[i], ln_a_b[i]))
        y_a = jnp.einsum("bsc,cd->bsd", a, w_a_out[i])

        s = sc_b * causal_dwconv(sc_c * sc_v, conv_b_w[i])
        y_b = jnp.einsum("bsc,cd->bsd", s, w_b_out[i])

        x = x + jnp.einsum("bsd,de->bse", g_a * y_a + g_b * y_b, w_o[i])

        h = rmsnorm(x, g_ffn[i])
        u = jnp.einsum("bsd,df->bsf", h, w_up[i])
        f_gate = causal_dwconv(u[..., :D_FF], conv_f_w[i]) + conv_f_b[i]
        f = jax.nn.gelu(f_gate, approximate=True) * u[..., D_FF:]
        x = x + jnp.einsum("bsf,fd->bsd", f, w_down[i])

        pg = jax.nn.sigmoid(jnp.einsum("bsd,de->bse", rmsnorm(x, g_ple[i]), w_ple_gate[i]))
        x = x + pg * jnp.einsum("bsk,kd->bsd", p[i], w_ple[i])

    return rmsnorm(x, g_final)
```

```python
import functools

import jax
import jax.numpy as jnp
from jax import lax
from jax.experimental import pallas as pl
from jax.experimental.pallas import tpu as pltpu

EPS = 1e-6
SUBLANES = 8
ROW_TILE = 256
CONV_ROWS = 16
FFN_COLS = 512
VMEM_LIMIT_BYTES = 56 * 1024 * 1024


def _round_up(n, m):
    return (n + m - 1) // m * m


def _dot(a, b):
    return jnp.dot(a, b, preferred_element_type=jnp.float32)


def _rmsnorm(x, g):
    y = x * lax.rsqrt(jnp.mean(x * x, axis=-1, keepdims=True) + EPS)
    return y * g


def _carry_halo(buf, halo, rows, first):
    @pl.when(first)
    def _():
        buf[0:halo, :] = jnp.zeros((halo, buf.shape[1]), buf.dtype)

    @pl.when(jnp.logical_not(first))
    def _():
        buf[0:halo, :] = buf[rows:rows + halo, :]


def _mixer_kernel(x_ref, g_ref, win_ref, bg_ref, caw_ref, cab_ref, lng_ref, lnb_ref,
                  wa_ref, cbw_ref, wb_ref, wo_ref, o_ref, abuf, cvbuf, *, tiles_per_seq):
    rows, dm = x_ref.shape
    ka, dc = caw_ref.shape
    kb, ds = cbw_ref.shape
    halo_a = abuf.shape[0] - rows
    halo_b = cvbuf.shape[0] - rows
    first = (pl.program_id(0) % tiles_per_seq) == 0
    _carry_halo(abuf, halo_a, rows, first)
    _carry_halo(cvbuf, halo_b, rows, first)

    x = x_ref[...]
    h = _rmsnorm(x, g_ref[...]).astype(jnp.bfloat16)

    za = _dot(h, win_ref[:, 0:2 * dc])
    abuf[halo_a:halo_a + rows, :] = za[:, :dc] * jax.nn.sigmoid(za[:, dc:])
    cab = cab_ref[...]
    lng = lng_ref[...]
    lnb = lnb_ref[...]
    act = []
    for r0 in range(0, rows, CONV_ROWS):
        acc = jnp.broadcast_to(cab, (CONV_ROWS, dc))
        for k in range(ka):
            s = halo_a + r0 - (ka - 1) + k
            acc = acc + caw_ref[k:k + 1, :] * abuf[s:s + CONV_ROWS, :]
        mu = jnp.mean(acc, axis=-1, keepdims=True)
        cen = acc - mu
        var = jnp.mean(cen * cen, axis=-1, keepdims=True)
        y = cen * lax.rsqrt(var + EPS) * lng + lnb
        act.append((y * jax.nn.sigmoid(y)).astype(jnp.bfloat16))
    y_a = _dot(jnp.concatenate(act, axis=0), wa_ref[...])

    o2 = 2 * dc
    zs = _dot(h, win_ref[:, o2:o2 + 3 * ds])
    cvbuf[halo_b:halo_b + rows, :] = zs[:, ds:2 * ds] * zs[:, 2 * ds:]
    conv_b = cbw_ref[0:1, :] * cvbuf[halo_b - (kb - 1):halo_b - (kb - 1) + rows, :]
    for k in range(1, kb):
        s = halo_b - (kb - 1) + k
        conv_b = conv_b + cbw_ref[k:k + 1, :] * cvbuf[s:s + rows, :]
    y_b = _dot((zs[:, :ds] * conv_b).astype(jnp.bfloat16), wb_ref[...])

    o5 = o2 + 3 * ds
    gates = jax.nn.sigmoid(_dot(h, win_ref[:, o5:]) + bg_ref[...])
    merged = gates[:, :dm] * y_a + gates[:, dm:] * y_b
    o_ref[...] = x + _dot(merged.astype(jnp.bfloat16), wo_ref[...])


def _ffn_kernel(x_ref, p_ref, gf_ref, wup_ref, cfw_ref, cfb_ref, wdn_ref, gp_ref, wple_ref,
                wpg_ref, gfin_ref, o_ref, gbuf, fbuf, *, tiles_per_seq, final_norm):
    rows, dm = x_ref.shape
    kf, dff = cfw_ref.shape
    halo = gbuf.shape[0] - rows
    first = (pl.program_id(0) % tiles_per_seq) == 0
    _carry_halo(gbuf, halo, rows, first)

    x = x_ref[...]
    h = _rmsnorm(x, gf_ref[...]).astype(jnp.bfloat16)
    for c0 in range(0, dff, FFN_COLS):
        c1 = min(c0 + FFN_COLS, dff)
        gbuf[halo:halo + rows, c0:c1] = _dot(h, wup_ref[:, c0:c1])
        conv = jnp.broadcast_to(cfb_ref[:, c0:c1], (rows, c1 - c0))
        for k in range(kf):
            s = halo - (kf - 1) + k
            conv = conv + cfw_ref[k:k + 1, c0:c1] * gbuf[s:s + rows, c0:c1]
        val = _dot(h, wup_ref[:, dff + c0:dff + c1])
        fbuf[:, c0:c1] = (jax.nn.gelu(conv, approximate=True) * val).astype(jnp.bfloat16)
    x = x + _dot(fbuf[...], wdn_ref[...])

    pg = jax.nn.sigmoid(_dot(_rmsnorm(x, gp_ref[...]).astype(jnp.bfloat16), wpg_ref[...]))
    x = x + pg * _dot(p_ref[...].astype(jnp.bfloat16), wple_ref[...])
    if final_norm:
        x = _rmsnorm(x, gfin_ref[...])
    o_ref[...] = x


def _resident(shape):
    return pl.BlockSpec(shape, lambda i: (0,) * len(shape), pipeline_mode=pl.Buffered(1))


def _row_tiles(cols):
    return pl.BlockSpec((ROW_TILE, cols), lambda i: (i, 0))


def _compiler_params():
    return pltpu.CompilerParams(dimension_semantics=("arbitrary",),
                                vmem_limit_bytes=VMEM_LIMIT_BYTES)


def _mixer_call(x, seq, g, w_in, b_gate, caw, cab, lng, lnb, wa, cbw, wb, wo):
    n, dm = x.shape
    ka, dc = caw.shape
    kb, ds = cbw.shape
    consts = (g, w_in, b_gate, caw, cab, lng, lnb, wa, cbw, wb, wo)
    return pl.pallas_call(
        functools.partial(_mixer_kernel, tiles_per_seq=seq // ROW_TILE),
        grid=(n // ROW_TILE,),
        in_specs=[_row_tiles(dm)] + [_resident(c.shape) for c in consts],
        out_specs=_row_tiles(dm),
        out_shape=jax.ShapeDtypeStruct((n, dm), x.dtype),
        scratch_shapes=[
            pltpu.VMEM((ROW_TILE + _round_up(ka - 1, SUBLANES), dc), jnp.float32),
            pltpu.VMEM((ROW_TILE + _round_up(kb - 1, SUBLANES), ds), jnp.float32),
        ],
        compiler_params=_compiler_params(),
        name="token_mixer",
    )(x, *consts)


def _ffn_call(x, p, seq, gf, wup, cfw, cfb, wdn, gp, wple, wpg, gfin, final_norm):
    n, dm = x.shape
    kf, dff = cfw.shape
    consts = (gf, wup, cfw, cfb, wdn, gp, wple, wpg, gfin)
    return pl.pallas_call(
        functools.partial(_ffn_kernel, tiles_per_seq=seq // ROW_TILE, final_norm=final_norm),
        grid=(n // ROW_TILE,),
        in_specs=[_row_tiles(dm), _row_tiles(p.shape[1])] + [_resident(c.shape) for c in consts],
        out_specs=_row_tiles(dm),
        out_shape=jax.ShapeDtypeStruct((n, dm), x.dtype),
        scratch_shapes=[
            pltpu.VMEM((ROW_TILE + _round_up(kf - 1, SUBLANES), dff), jnp.float32),
            pltpu.VMEM((ROW_TILE, dff), jnp.bfloat16),
        ],
        compiler_params=_compiler_params(),
        name="channel_mixer",
    )(x, p, *consts)


def kernel(x, p, g_mix, w_in, b_gate, conv_a_w, conv_a_b, ln_a_g, ln_a_b, w_a_out, conv_b_w, w_b_out, w_o, g_ffn, w_up, conv_f_w, conv_f_b, w_down, g_ple, w_ple, w_ple_gate, g_final):
    batch, seq, dm = x.shape
    depth = p.shape[0]
    assert seq % ROW_TILE == 0
    bf = lambda w: w.astype(jnp.bfloat16)
    row = lambda v: v.reshape(1, -1)
    xf = x.reshape(batch * seq, dm)
    for i in range(depth):
        xf = _mixer_call(xf, seq, row(g_mix[i]), bf(w_in[i]), row(b_gate[i]), conv_a_w[i],
                         row(conv_a_b[i]), row(ln_a_g[i]), row(ln_a_b[i]), bf(w_a_out[i]),
                         conv_b_w[i], bf(w_b_out[i]), bf(w_o[i]))
        xf = _ffn_call(xf, p[i].reshape(batch * seq, -1), seq, row(g_ffn[i]), bf(w_up[i]),
                       conv_f_w[i], row(conv_f_b[i]), bf(w_down[i]), row(g_ple[i]), bf(w_ple[i]),
                       bf(w_ple_gate[i]), row(g_final), final_norm=(i == depth - 1))
    return xf.reshape(batch, seq, dm)
```

```python
import functools

import jax
import jax.numpy as jnp
from jax import lax
from jax.experimental import pallas as pl
from jax.experimental.pallas import tpu as pltpu

EPS = 1e-6
SUBLANES = 8
ROW_TILE = 256
NORM_ROWS = 16
CONV_ROWS = 32
MXU_COLS = 512
VMEM_LIMIT_BYTES = 56 * 1024 * 1024

F32 = jnp.float32
BF16 = jnp.bfloat16


def _round_up(n, m):
    return (n + m - 1) // m * m


def _dot(a, b):
    return jnp.dot(a, b, preferred_element_type=F32)


def _rmsnorm(x, g):
    y = x * lax.rsqrt(jnp.mean(x * x, axis=-1, keepdims=True) + EPS)
    return y * g


def _rmsnorm_rows(src_ref, g_ref, dst_ref):
    g = g_ref[...]
    for r0 in range(0, src_ref.shape[0], NORM_ROWS):
        dst_ref[r0:r0 + NORM_ROWS, :] = _rmsnorm(src_ref[r0:r0 + NORM_ROWS, :], g).astype(dst_ref.dtype)


class _Taps:
    def __init__(self, k_taps, cur_in_copies):
        self.halo = _round_up(k_taps - 1, SUBLANES)
        self.qr = [divmod(self.halo - (k_taps - 1) + k, SUBLANES) for k in range(k_taps)]
        self.cur_in_copies = cur_in_copies
        self.residues = sorted({r for q, r in self.qr if cur_in_copies or not self._is_cur(q, r)})

    def _is_cur(self, q, r):
        return (q * SUBLANES, r) == (self.halo, 0)

    def scratch(self, rows, cols):
        return pltpu.VMEM((len(self.residues), self.halo + rows, cols), F32)

    def carry(self, buf, rows, first):
        @pl.when(first)
        def _():
            buf[...] = jnp.zeros(buf.shape, buf.dtype)

        @pl.when(jnp.logical_not(first))
        def _():
            buf[:, 0:self.halo, :] = buf[:, rows:rows + self.halo, :]

    def store(self, buf, value, c0=0):
        rows, cols = value.shape
        for slot, r in enumerate(self.residues):
            buf[slot, self.halo - r:self.halo - r + rows, c0:c0 + cols] = value

    def tap(self, buf, k, r0, nrows, c0, c1, cur=None):
        q, r = self.qr[k]
        if self._is_cur(q, r) and not self.cur_in_copies:
            return cur
        start = r0 + q * SUBLANES
        return buf[self.residues.index(r), start:start + nrows, c0:c1]


_TAPS_A = functools.partial(_Taps, cur_in_copies=True)
_TAPS_3 = functools.partial(_Taps, cur_in_copies=False)


def _mixer_kernel(x_ref, g_ref, wglu_ref, wmid_ref, bg_ref, caw_ref, cab_ref, lng_ref, lnb_ref,
                  wa_ref, cbw_ref, wb_ref, wo_ref, o_ref,
                  abuf, cvbuf, h_ref, t0_ref, mid_ref, act_ref, s_ref, m_ref, mb_ref,
                  *, tiles_per_seq):
    rows, dm = x_ref.shape
    ka, dc = caw_ref.shape[0] // SUBLANES, caw_ref.shape[1]
    kb, ds = cbw_ref.shape
    n_mid = wmid_ref.shape[0]
    i_b, i_c, i_v = 0, ds // MXU_COLS, 2 * ds // MXU_COLS
    i_ga, i_gb = 3 * ds // MXU_COLS, (3 * ds + dm) // MXU_COLS
    taps_a, taps_b = _TAPS_A(ka), _TAPS_3(kb)
    first = (pl.program_id(0) % tiles_per_seq) == 0
    taps_a.carry(abuf, rows, first)
    taps_b.carry(cvbuf, rows, first)

    _rmsnorm_rows(x_ref, g_ref, h_ref)

    t0_ref[...] = _dot(h_ref[...], wglu_ref[:, 0:dc])
    taps_a.store(abuf, t0_ref[...] * jax.nn.sigmoid(_dot(h_ref[...], wglu_ref[:, dc:])))
    cab, lng, lnb = cab_ref[...], lng_ref[...], lnb_ref[...]

    def conv_chunk(r0):
        groups = range(0, CONV_ROWS, SUBLANES)
        accs = [jnp.broadcast_to(cab, (SUBLANES, dc)) for _ in groups]
        for k in range(ka):
            wk = caw_ref[k * SUBLANES:(k + 1) * SUBLANES, :]
            for i, g in enumerate(groups):
                accs[i] = accs[i] + wk * taps_a.tap(abuf, k, r0 + g, SUBLANES, 0, dc)
        acc = jnp.concatenate(accs, axis=0)
        mu = jnp.mean(acc, axis=-1, keepdims=True)
        cen = acc - mu
        var = jnp.mean(cen * cen, axis=-1, keepdims=True)
        y = cen * lax.rsqrt(var + EPS) * lng + lnb
        act_ref[r0:r0 + CONV_ROWS, :] = (y * jax.nn.sigmoid(y)).astype(BF16)

    n_conv = rows // CONV_ROWS
    for i in range(max(n_mid, n_conv)):
        if i < n_conv:
            conv_chunk(i * CONV_ROWS)
        if i < n_mid:
            mid_ref[i] = _dot(h_ref[...], wmid_ref[i])

    for j in range(ds // MXU_COLS):
        c0, c1 = j * MXU_COLS, (j + 1) * MXU_COLS
        cv = mid_ref[i_c + j] * mid_ref[i_v + j]
        taps_b.store(cvbuf, cv, c0)
        conv = cbw_ref[0:1, c0:c1] * taps_b.tap(cvbuf, 0, 0, rows, c0, c1, cv)
        for k in range(1, kb):
            conv = conv + cbw_ref[k:k + 1, c0:c1] * taps_b.tap(cvbuf, k, 0, rows, c0, c1, cv)
        s_ref[:, c0:c1] = (mid_ref[i_b + j] * conv).astype(BF16)

    for j in range(dm // MXU_COLS):
        c0, c1 = j * MXU_COLS, (j + 1) * MXU_COLS
        g_a = jax.nn.sigmoid(mid_ref[i_ga + j] + bg_ref[:, c0:c1])
        m_ref[:, c0:c1] = g_a * _dot(act_ref[...], wa_ref[:, c0:c1])
    for j in range(dm // MXU_COLS):
        c0, c1 = j * MXU_COLS, (j + 1) * MXU_COLS
        g_b = jax.nn.sigmoid(mid_ref[i_gb + j] + bg_ref[:, dm + c0:dm + c1])
        mb_ref[:, c0:c1] = (m_ref[:, c0:c1] + g_b * _dot(s_ref[...], wb_ref[:, c0:c1])).astype(BF16)
    o_ref[...] = x_ref[...] + _dot(mb_ref[...], wo_ref[...])


def _mixer_scratch(rows, dm, ka, dc, kb, ds, n_mid):
    return [
        _TAPS_A(ka).scratch(rows, dc), _TAPS_3(kb).scratch(rows, ds),
        pltpu.VMEM((rows, dm), BF16),
        pltpu.VMEM((rows, dc), F32),
        pltpu.VMEM((n_mid, rows, MXU_COLS), F32),
        pltpu.VMEM((rows, dc), BF16),
        pltpu.VMEM((rows, ds), BF16),
        pltpu.VMEM((rows, dm), F32),
        pltpu.VMEM((rows, dm), BF16),
    ]


def _ffn_kernel(x_ref, p_ref, gf_ref, wup_ref, cfw_ref, cfb_ref, wdn_ref, gp_ref, wple_ref,
                wpg_ref, gfin_ref, o_ref, gbuf, h_ref, t0_ref, fbuf, pg_ref,
                *, tiles_per_seq, final_norm):
    rows, dm = x_ref.shape
    kf, dff = cfw_ref.shape
    taps = _TAPS_3(kf)
    first = (pl.program_id(0) % tiles_per_seq) == 0
    taps.carry(gbuf, rows, first)

    _rmsnorm_rows(x_ref, gf_ref, h_ref)
    for c0 in range(0, dff, MXU_COLS):
        c1 = min(c0 + MXU_COLS, dff)
        gate = _dot(h_ref[...], wup_ref[:, c0:c1])
        t0_ref[:, 0:c1 - c0] = gate
        taps.store(gbuf, gate, c0)
        conv = jnp.broadcast_to(cfb_ref[:, c0:c1], (rows, c1 - c0))
        for k in range(kf):
            conv = conv + cfw_ref[k:k + 1, c0:c1] * taps.tap(gbuf, k, 0, rows, c0, c1,
                                                            t0_ref[:, 0:c1 - c0])
        val = _dot(h_ref[...], wup_ref[:, dff + c0:dff + c1])
        fbuf[:, c0:c1] = (jax.nn.gelu(conv, approximate=True) * val).astype(BF16)
    o_ref[...] = x_ref[...] + _dot(fbuf[...], wdn_ref[...])

    _rmsnorm_rows(o_ref, gp_ref, h_ref)
    pg_ref[...] = jax.nn.sigmoid(_dot(h_ref[...], wpg_ref[...]))
    o_ref[...] = o_ref[...] + pg_ref[...] * _dot(p_ref[...].astype(BF16), wple_ref[...])
    if final_norm:
        _rmsnorm_rows(o_ref, gfin_ref, o_ref)


def _ffn_scratch(rows, dm, kf, dff):
    return [
        _TAPS_3(kf).scratch(rows, dff),
        pltpu.VMEM((rows, dm), BF16),
        pltpu.VMEM((rows, MXU_COLS), F32),
        pltpu.VMEM((rows, dff), BF16),
        pltpu.VMEM((rows, dm), F32),
    ]


def _resident(shape):
    return pl.BlockSpec(shape, lambda i: (0,) * len(shape), pipeline_mode=pl.Buffered(1))


def _row_tiles(cols):
    return pl.BlockSpec((ROW_TILE, cols), lambda i: (i, 0))


def _compiler_params():
    return pltpu.CompilerParams(dimension_semantics=("arbitrary",),
                                vmem_limit_bytes=VMEM_LIMIT_BYTES)


def _mixer_call(x, seq, g, w_in, b_gate, caw, cab, lng, lnb, wa, cbw, wb, wo):
    n, dm = x.shape
    ka, dc = caw.shape
    kb, ds = cbw.shape
    assert ds % MXU_COLS == 0 and dm % MXU_COLS == 0
    caw = jnp.repeat(caw, SUBLANES, axis=0)
    w_glu = w_in[:, :2 * dc]
    n_mid = (w_in.shape[1] - 2 * dc) // MXU_COLS
    w_mid = w_in[:, 2 * dc:].reshape(dm, n_mid, MXU_COLS).transpose(1, 0, 2)
    consts = (g, w_glu, w_mid, b_gate, caw, cab, lng, lnb, wa, cbw, wb, wo)
    return pl.pallas_call(
        functools.partial(_mixer_kernel, tiles_per_seq=seq // ROW_TILE),
        grid=(n // ROW_TILE,),
        in_specs=[_row_tiles(dm)] + [_resident(c.shape) for c in consts],
        out_specs=_row_tiles(dm),
        out_shape=jax.ShapeDtypeStruct((n, dm), x.dtype),
        scratch_shapes=_mixer_scratch(ROW_TILE, dm, ka, dc, kb, ds, n_mid),
        compiler_params=_compiler_params(),
        name="token_mixer",
    )(x, *consts)


def _ffn_call(x, p, seq, gf, wup, cfw, cfb, wdn, gp, wple, wpg, gfin, final_norm):
    n, dm = x.shape
    kf, dff = cfw.shape
    consts = (gf, wup, cfw, cfb, wdn, gp, wple, wpg, gfin)
    return pl.pallas_call(
        functools.partial(_ffn_kernel, tiles_per_seq=seq // ROW_TILE, final_norm=final_norm),
        grid=(n // ROW_TILE,),
        in_specs=[_row_tiles(dm), _row_tiles(p.shape[1])] + [_resident(c.shape) for c in consts],
        out_specs=_row_tiles(dm),
        out_shape=jax.ShapeDtypeStruct((n, dm), x.dtype),
        scratch_shapes=_ffn_scratch(ROW_TILE, dm, kf, dff),
        compiler_params=_compiler_params(),
        name="channel_mixer",
    )(x, p, *consts)


def kernel(x, p, g_mix, w_in, b_gate, conv_a_w, conv_a_b, ln_a_g, ln_a_b, w_a_out, conv_b_w, w_b_out, w_o, g_ffn, w_up, conv_f_w, conv_f_b, w_down, g_ple, w_ple, w_ple_gate, g_final):
    batch, seq, dm = x.shape
    depth = p.shape[0]
    assert seq % ROW_TILE == 0
    bf = lambda w: w.astype(BF16)
    row = lambda v: v.reshape(1, -1)
    xf = x.reshape(batch * seq, dm)
    for i in range(depth):
        xf = _mixer_call(xf, seq, row(g_mix[i]), bf(w_in[i]), row(b_gate[i]), conv_a_w[i],
                         row(conv_a_b[i]), row(ln_a_g[i]), row(ln_a_b[i]), bf(w_a_out[i]),
                         conv_b_w[i], bf(w_b_out[i]), bf(w_o[i]))
        xf = _ffn_call(xf, p[i].reshape(batch * seq, -1), seq, row(g_ffn[i]), bf(w_up[i]),
                       conv_f_w[i], row(conv_f_b[i]), bf(w_down[i]), row(g_ple[i]), bf(w_ple[i]),
                       bf(w_ple_gate[i]), row(g_final), final_norm=(i == depth - 1))
    return xf.reshape(batch, seq, dm)
```

```python
import functools
import itertools

import jax
import jax.numpy as jnp
from jax import lax
from jax.experimental import pallas as pl
from jax.experimental.pallas import tpu as pltpu

EPS = 1e-6
SUBLANES = 8
LANES = 128
ROW_TILE = 512
NORM_ROWS = 16
CONV_ROWS = 64
MXU_COLS = 512
MID_PER_BLOCK = (2, 2, 2, 1)
VMEM_LIMIT_BYTES = 56 * 1024 * 1024

F32 = jnp.float32
BF16 = jnp.bfloat16


def _round_up(n, m):
    return (n + m - 1) // m * m


def _dot(a, b):
    return jnp.dot(a, b, preferred_element_type=F32)


def _rmsnorm(x, g):
    y = x * lax.rsqrt(jnp.mean(x * x, axis=-1, keepdims=True) + EPS)
    return y * g


def _rmsnorm_rows(src_ref, g_ref, dst_ref):
    g = g_ref[...]
    for r0 in range(0, src_ref.shape[0], NORM_ROWS):
        dst_ref[r0:r0 + NORM_ROWS, :] = _rmsnorm(src_ref[r0:r0 + NORM_ROWS, :], g).astype(dst_ref.dtype)


class _Taps:
    def __init__(self, k_taps, cur_in_copies):
        self.halo = _round_up(k_taps - 1, SUBLANES)
        self.qr = [divmod(self.halo - (k_taps - 1) + k, SUBLANES) for k in range(k_taps)]
        self.cur_in_copies = cur_in_copies
        self.residues = sorted({r for q, r in self.qr if cur_in_copies or not self._is_cur(q, r)})

    def _is_cur(self, q, r):
        return (q * SUBLANES, r) == (self.halo, 0)

    def scratch(self, rows, cols):
        return pltpu.VMEM((len(self.residues), self.halo + rows, cols), F32)

    def carry(self, buf, rows, first):
        @pl.when(first)
        def _():
            buf[...] = jnp.zeros(buf.shape, buf.dtype)

        @pl.when(jnp.logical_not(first))
        def _():
            buf[:, 0:self.halo, :] = buf[:, rows:rows + self.halo, :]

    def store(self, buf, value, c0=0):
        rows, cols = value.shape
        for slot, r in enumerate(self.residues):
            buf[slot, self.halo - r:self.halo - r + rows, c0:c0 + cols] = value

    def tap(self, buf, k, r0, nrows, c0, c1, cur=None):
        q, r = self.qr[k]
        if self._is_cur(q, r) and not self.cur_in_copies:
            return cur
        start = r0 + q * SUBLANES
        return buf[self.residues.index(r), start:start + nrows, c0:c1]


_TAPS_A = functools.partial(_Taps, cur_in_copies=True)
_TAPS_3 = functools.partial(_Taps, cur_in_copies=False)


def _mixer_kernel(x_ref, g_ref, wglu_ref, wmid_ref, bg_ref, caw_ref, cab_ref, lng_ref, lnb_ref,
                  wa_ref, cbw_ref, wb_ref, wo_ref, o_ref,
                  abuf, cvbuf, h_ref, conv_ref, mid_ref, act_ref, s_ref, m_ref, mb_ref,
                  *, tiles_per_seq):
    rows, dm = x_ref.shape
    ka, dc = caw_ref.shape[0] // SUBLANES, caw_ref.shape[1]
    kb, ds = cbw_ref.shape
    n_mid = wmid_ref.shape[0]
    i_b, i_c, i_v = 0, ds // MXU_COLS, 2 * ds // MXU_COLS
    i_ga, i_gb = 3 * ds // MXU_COLS, (3 * ds + dm) // MXU_COLS
    taps_a, taps_b = _TAPS_A(ka), _TAPS_3(kb)
    first = (pl.program_id(0) % tiles_per_seq) == 0
    taps_a.carry(abuf, rows, first)
    taps_b.carry(cvbuf, rows, first)

    _rmsnorm_rows(x_ref, g_ref, h_ref)

    n_blocks = dc // LANES
    n_groups = CONV_ROWS // SUBLANES

    def glu_dot(b):
        return _dot(h_ref[...], wglu_ref[:, 2 * b * LANES:2 * (b + 1) * LANES])

    def glu_store(b, z):
        taps_a.store(abuf, z[:, :LANES] * jax.nn.sigmoid(z[:, LANES:]), b * LANES)

    def conv_block(b):
        lanes = slice(b * LANES, (b + 1) * LANES)
        w = [caw_ref[k * SUBLANES:(k + 1) * SUBLANES, lanes] for k in range(ka)]
        bias = jnp.broadcast_to(cab_ref[:, lanes], (SUBLANES, LANES))
        for r0 in range(0, rows, CONV_ROWS):
            accs = [bias] * n_groups
            for slot, r in enumerate(taps_a.residues):
                qk = [(q, k) for k, (q, rr) in enumerate(taps_a.qr) if rr == r]
                for m in range(min(q for q, _ in qk), n_groups + max(q for q, _ in qk)):
                    win = abuf[slot, r0 + m * SUBLANES:r0 + (m + 1) * SUBLANES, lanes]
                    for q, k in qk:
                        if 0 <= m - q < n_groups:
                            accs[m - q] = accs[m - q] + w[k] * win
            conv_ref[r0:r0 + CONV_ROWS, lanes] = jnp.concatenate(accs, axis=0)

    def layernorm_swish():
        lng, lnb = lng_ref[...], lnb_ref[...]
        for r0 in range(0, rows, NORM_ROWS):
            acc = conv_ref[r0:r0 + NORM_ROWS, :]
            mu = jnp.mean(acc, axis=-1, keepdims=True)
            cen = acc - mu
            var = jnp.mean(cen * cen, axis=-1, keepdims=True)
            y = cen * lax.rsqrt(var + EPS) * lng + lnb
            act_ref[r0:r0 + NORM_ROWS, :] = (y * jax.nn.sigmoid(y)).astype(BF16)

    mid = iter(range(n_mid))

    def mid_dots(count):
        for i in itertools.islice(mid, count):
            mid_ref[i] = _dot(h_ref[...], wmid_ref[i])

    for b in range(n_blocks):
        glu_store(b, glu_dot(b))
        mid_dots(MID_PER_BLOCK[b])
        conv_block(b)
    mid_dots(n_mid)
    layernorm_swish()

    for j in range(ds // MXU_COLS):
        c0, c1 = j * MXU_COLS, (j + 1) * MXU_COLS
        cv = mid_ref[i_c + j] * mid_ref[i_v + j]
        taps_b.store(cvbuf, cv, c0)
        conv = cbw_ref[0:1, c0:c1] * taps_b.tap(cvbuf, 0, 0, rows, c0, c1, cv)
        for k in range(1, kb):
            conv = conv + cbw_ref[k:k + 1, c0:c1] * taps_b.tap(cvbuf, k, 0, rows, c0, c1, cv)
        s_ref[:, c0:c1] = (mid_ref[i_b + j] * conv).astype(BF16)

    for j in range(dm // MXU_COLS):
        c0, c1 = j * MXU_COLS, (j + 1) * MXU_COLS
        g_a = jax.nn.sigmoid(mid_ref[i_ga + j] + bg_ref[:, c0:c1])
        m_ref[:, c0:c1] = g_a * _dot(act_ref[...], wa_ref[:, c0:c1])
    for j in range(dm // MXU_COLS):
        c0, c1 = j * MXU_COLS, (j + 1) * MXU_COLS
        g_b = jax.nn.sigmoid(mid_ref[i_gb + j] + bg_ref[:, dm + c0:dm + c1])
        mb_ref[:, c0:c1] = (m_ref[:, c0:c1] + g_b * _dot(s_ref[...], wb_ref[:, c0:c1])).astype(BF16)
    o_ref[...] = x_ref[...] + _dot(mb_ref[...], wo_ref[...])


def _mixer_scratch(rows, dm, ka, dc, kb, ds, n_mid):
    return [
        _TAPS_A(ka).scratch(rows, dc), _TAPS_3(kb).scratch(rows, ds),
        pltpu.VMEM((rows, dm), BF16),
        pltpu.VMEM((rows, dc), F32),
        pltpu.VMEM((n_mid, rows, MXU_COLS), F32),
        pltpu.VMEM((rows, dc), BF16),
        pltpu.VMEM((rows, ds), BF16),
        pltpu.VMEM((rows, dm), F32),
        pltpu.VMEM((rows, dm), BF16),
    ]


def _ffn_kernel(x_ref, p_ref, gf_ref, wup_ref, cfw_ref, cfb_ref, wdn_ref, gp_ref, wple_ref,
                wpg_ref, gfin_ref, o_ref, gbuf, h_ref, t0_ref, fbuf, pg_ref,
                *, tiles_per_seq, final_norm):
    rows, dm = x_ref.shape
    kf, dff = cfw_ref.shape
    taps = _TAPS_3(kf)
    first = (pl.program_id(0) % tiles_per_seq) == 0
    taps.carry(gbuf, rows, first)

    _rmsnorm_rows(x_ref, gf_ref, h_ref)
    for c0 in range(0, dff, MXU_COLS):
        c1 = min(c0 + MXU_COLS, dff)
        gate = _dot(h_ref[...], wup_ref[:, c0:c1])
        t0_ref[:, 0:c1 - c0] = gate
        taps.store(gbuf, gate, c0)
        conv = jnp.broadcast_to(cfb_ref[:, c0:c1], (rows, c1 - c0))
        for k in range(kf):
            conv = conv + cfw_ref[k:k + 1, c0:c1] * taps.tap(gbuf, k, 0, rows, c0, c1,
                                                            t0_ref[:, 0:c1 - c0])
        val = _dot(h_ref[...], wup_ref[:, dff + c0:dff + c1])
        fbuf[:, c0:c1] = (jax.nn.gelu(conv, approximate=True) * val).astype(BF16)
    o_ref[...] = x_ref[...] + _dot(fbuf[...], wdn_ref[...])

    _rmsnorm_rows(o_ref, gp_ref, h_ref)
    pg_ref[...] = jax.nn.sigmoid(_dot(h_ref[...], wpg_ref[...]))
    o_ref[...] = o_ref[...] + pg_ref[...] * _dot(p_ref[...].astype(BF16), wple_ref[...])
    if final_norm:
        _rmsnorm_rows(o_ref, gfin_ref, o_ref)


def _ffn_scratch(rows, dm, kf, dff):
    return [
        _TAPS_3(kf).scratch(rows, dff),
        pltpu.VMEM((rows, dm), BF16),
        pltpu.VMEM((rows, MXU_COLS), F32),
        pltpu.VMEM((rows, dff), BF16),
        pltpu.VMEM((rows, dm), F32),
    ]


def _resident(shape):
    return pl.BlockSpec(shape, lambda i: (0,) * len(shape), pipeline_mode=pl.Buffered(1))


def _row_tiles(cols):
    return pl.BlockSpec((ROW_TILE, cols), lambda i: (i, 0))


def _compiler_params():
    return pltpu.CompilerParams(dimension_semantics=("arbitrary",),
                                vmem_limit_bytes=VMEM_LIMIT_BYTES)


def _mixer_call(x, seq, g, w_in, b_gate, caw, cab, lng, lnb, wa, cbw, wb, wo):
    n, dm = x.shape
    ka, dc = caw.shape
    kb, ds = cbw.shape
    assert ds % MXU_COLS == 0 and dm % MXU_COLS == 0
    caw = jnp.repeat(caw, SUBLANES, axis=0)
    w_glu = w_in[:, :2 * dc].reshape(dm, 2, dc // LANES, LANES).transpose(0, 2, 1, 3)
    w_glu = w_glu.reshape(dm, 2 * dc).astype(BF16)
    n_mid = (w_in.shape[1] - 2 * dc) // MXU_COLS
    w_mid = w_in[:, 2 * dc:].reshape(dm, n_mid, MXU_COLS).transpose(1, 0, 2).astype(BF16)
    consts = (g, w_glu, w_mid, b_gate, caw, cab, lng, lnb, wa, cbw, wb, wo)
    return pl.pallas_call(
        functools.partial(_mixer_kernel, tiles_per_seq=seq // ROW_TILE),
        grid=(n // ROW_TILE,),
        in_specs=[_row_tiles(dm)] + [_resident(c.shape) for c in consts],
        out_specs=_row_tiles(dm),
        out_shape=jax.ShapeDtypeStruct((n, dm), x.dtype),
        scratch_shapes=_mixer_scratch(ROW_TILE, dm, ka, dc, kb, ds, n_mid),
        compiler_params=_compiler_params(),
        name="token_mixer",
    )(x, *consts)


def _ffn_call(x, p, seq, gf, wup, cfw, cfb, wdn, gp, wple, wpg, gfin, final_norm):
    n, dm = x.shape
    kf, dff = cfw.shape
    consts = (gf, wup, cfw, cfb, wdn, gp, wple, wpg, gfin)
    return pl.pallas_call(
        functools.partial(_ffn_kernel, tiles_per_seq=seq // ROW_TILE, final_norm=final_norm),
        grid=(n // ROW_TILE,),
        in_specs=[_row_tiles(dm), _row_tiles(p.shape[1])] + [_resident(c.shape) for c in consts],
        out_specs=_row_tiles(dm),
        out_shape=jax.ShapeDtypeStruct((n, dm), x.dtype),
        scratch_shapes=_ffn_scratch(ROW_TILE, dm, kf, dff),
        compiler_params=_compiler_params(),
        name="channel_mixer",
    )(x, p, *consts)


def kernel(x, p, g_mix, w_in, b_gate, conv_a_w, conv_a_b, ln_a_g, ln_a_b, w_a_out, conv_b_w, w_b_out, w_o, g_ffn, w_up, conv_f_w, conv_f_b, w_down, g_ple, w_ple, w_ple_gate, g_final):
    batch, seq, dm = x.shape
    depth = p.shape[0]
    assert seq % ROW_TILE == 0
    bf = lambda w: w.astype(BF16)
    row = lambda v: v.reshape(1, -1)
    xf = x.reshape(batch * seq, dm)
    for i in range(depth):
        xf = _mixer_call(xf, seq, row(g_mix[i]), w_in[i], row(b_gate[i]), conv_a_w[i],
                         row(conv_a_b[i]), row(ln_a_g[i]), row(ln_a_b[i]), bf(w_a_out[i]),
                         conv_b_w[i], bf(w_b_out[i]), bf(w_o[i]))
        xf = _ffn_call(xf, p[i].reshape(batch * seq, -1), seq, row(g_ffn[i]), bf(w_up[i]),
                       conv_f_w[i], row(conv_f_b[i]), bf(w_down[i]), row(g_ple[i]), bf(w_ple[i]),
                       bf(w_ple_gate[i]), row(g_final), final_norm=(i == depth - 1))
    return xf.reshape(batch, seq, dm)
```

```python
import functools
import itertools

import jax
import jax.numpy as jnp
from jax import lax
from jax.experimental import pallas as pl
from jax.experimental.pallas import tpu as pltpu

EPS = 1e-6
SUBLANES = 8
BF16_ROWS = 16
LANES = 128
ROW_TILE = 512
WEIGHT_STEPS = 16
NORM_ROWS = 16
CONV_ROWS = 64
MXU_COLS = 512
MID_PER_BLOCK = (2, 2, 2, 1)
VMEM_LIMIT_BYTES = 58 * 1024 * 1024

F32 = jnp.float32
BF16 = jnp.bfloat16


def _round_up(n, m):
    return (n + m - 1) // m * m


def _dot(a, b):
    return jnp.dot(a, b, preferred_element_type=F32)


def _rmsnorm(x, g):
    y = x * lax.rsqrt(jnp.mean(x * x, axis=-1, keepdims=True) + EPS)
    return y * g


def _rmsnorm_rows(src_ref, g_ref, dst_ref):
    g = g_ref[...]
    for r0 in range(0, src_ref.shape[0], NORM_ROWS):
        dst_ref[r0:r0 + NORM_ROWS, :] = _rmsnorm(src_ref[r0:r0 + NORM_ROWS, :], g).astype(dst_ref.dtype)


def _cast_rows(step, src_ref, dst_ref, src_cols=None, dst_col0=0):
    chunk = src_ref.shape[0]
    assert chunk % BF16_ROWS == 0
    rows = pl.ds(pl.multiple_of(step * chunk, chunk), chunk)
    c0, c1 = src_cols or (0, src_ref.shape[1])
    for c in range(c0, c1, MXU_COLS):
        w = min(MXU_COLS, c1 - c)
        dst_ref[rows, dst_col0 + c - c0:dst_col0 + c - c0 + w] = src_ref[:, c:c + w].astype(BF16)


class _Taps:
    def __init__(self, k_taps, cur_in_copies):
        self.halo = _round_up(k_taps - 1, SUBLANES)
        self.qr = [divmod(self.halo - (k_taps - 1) + k, SUBLANES) for k in range(k_taps)]
        self.cur_in_copies = cur_in_copies
        self.residues = sorted({r for q, r in self.qr if cur_in_copies or not self._is_cur(q, r)})

    def _is_cur(self, q, r):
        return (q * SUBLANES, r) == (self.halo, 0)

    def scratch(self, rows, cols):
        return pltpu.VMEM((len(self.residues), self.halo + rows, cols), F32)

    def carry(self, buf, rows, first):
        @pl.when(first)
        def _():
            buf[...] = jnp.zeros(buf.shape, buf.dtype)

        @pl.when(jnp.logical_not(first))
        def _():
            buf[:, 0:self.halo, :] = buf[:, rows:rows + self.halo, :]

    def store(self, buf, value, c0=0):
        rows, cols = value.shape
        for slot, r in enumerate(self.residues):
            buf[slot, self.halo - r:self.halo - r + rows, c0:c0 + cols] = value

    def tap(self, buf, k, r0, nrows, c0, c1, cur=None):
        q, r = self.qr[k]
        if self._is_cur(q, r) and not self.cur_in_copies:
            return cur
        start = r0 + q * SUBLANES
        return buf[self.residues.index(r), start:start + nrows, c0:c1]


_TAPS_A = functools.partial(_Taps, cur_in_copies=True)
_TAPS_3 = functools.partial(_Taps, cur_in_copies=False)


def _mixer_kernel(x_ref, g_ref, win_ref, bg_ref, caw_ref, cab_ref, lng_ref, lnb_ref,
                  wa_ref, cbw_ref, wb_ref, wo_ref, o_ref,
                  wglu_s, wmid_s, wa_s, wb_s, wo_s,
                  abuf, cvbuf, h_ref, conv_ref, mid_ref, act_ref, s_ref, m_ref, mb_ref,
                  *, tiles_per_seq):
    rows, dm = x_ref.shape
    ka, dc = caw_ref.shape[0] // SUBLANES, caw_ref.shape[1]
    kb, ds = cbw_ref.shape
    n_blocks = dc // LANES
    n_mid = mid_ref.shape[0]
    i_b, i_c, i_v = 0, ds // MXU_COLS, 2 * ds // MXU_COLS
    i_ga, i_gb = 3 * ds // MXU_COLS, (3 * ds + dm) // MXU_COLS
    taps_a, taps_b = _TAPS_A(ka), _TAPS_3(kb)
    step = pl.program_id(0)

    @pl.when(step < WEIGHT_STEPS)
    def _():
        for b in range(n_blocks):
            _cast_rows(step, win_ref, wglu_s, (b * LANES, (b + 1) * LANES), 2 * b * LANES)
            _cast_rows(step, win_ref, wglu_s, (dc + b * LANES, dc + (b + 1) * LANES),
                       (2 * b + 1) * LANES)
        _cast_rows(step, win_ref, wmid_s, (2 * dc, win_ref.shape[1]))
        _cast_rows(step, wa_ref, wa_s)
        _cast_rows(step, wb_ref, wb_s)
        _cast_rows(step, wo_ref, wo_s)

    @pl.when(step >= WEIGHT_STEPS)
    def _():
        first = ((step - WEIGHT_STEPS) % tiles_per_seq) == 0
        taps_a.carry(abuf, rows, first)
        taps_b.carry(cvbuf, rows, first)

        _rmsnorm_rows(x_ref, g_ref, h_ref)

        n_groups = CONV_ROWS // SUBLANES

        def glu_block(b):
            z = _dot(h_ref[...], wglu_s[:, 2 * b * LANES:2 * (b + 1) * LANES])
            taps_a.store(abuf, z[:, :LANES] * jax.nn.sigmoid(z[:, LANES:]), b * LANES)

        def conv_block(b):
            lanes = slice(b * LANES, (b + 1) * LANES)
            w = [caw_ref[k * SUBLANES:(k + 1) * SUBLANES, lanes] for k in range(ka)]
            bias = jnp.broadcast_to(cab_ref[:, lanes], (SUBLANES, LANES))
            for r0 in range(0, rows, CONV_ROWS):
                accs = [bias] * n_groups
                for slot, r in enumerate(taps_a.residues):
                    qk = [(q, k) for k, (q, rr) in enumerate(taps_a.qr) if rr == r]
                    for m in range(min(q for q, _ in qk), n_groups + max(q for q, _ in qk)):
                        win = abuf[slot, r0 + m * SUBLANES:r0 + (m + 1) * SUBLANES, lanes]
                        for q, k in qk:
                            if 0 <= m - q < n_groups:
                                accs[m - q] = accs[m - q] + w[k] * win
                conv_ref[r0:r0 + CONV_ROWS, lanes] = jnp.concatenate(accs, axis=0)

        def layernorm_swish():
            lng, lnb = lng_ref[...], lnb_ref[...]
            for r0 in range(0, rows, NORM_ROWS):
                acc = conv_ref[r0:r0 + NORM_ROWS, :]
                mu = jnp.mean(acc, axis=-1, keepdims=True)
                cen = acc - mu
                var = jnp.mean(cen * cen, axis=-1, keepdims=True)
                y = cen * lax.rsqrt(var + EPS) * lng + lnb
                act_ref[r0:r0 + NORM_ROWS, :] = (y * jax.nn.sigmoid(y)).astype(BF16)

        mid = iter(range(n_mid))

        def mid_dots(count):
            for i in itertools.islice(mid, count):
                mid_ref[i] = _dot(h_ref[...], wmid_s[:, i * MXU_COLS:(i + 1) * MXU_COLS])

        for b in range(n_blocks):
            glu_block(b)
            mid_dots(MID_PER_BLOCK[b])
            conv_block(b)
        mid_dots(n_mid)
        layernorm_swish()

        for j in range(ds // MXU_COLS):
            c0, c1 = j * MXU_COLS, (j + 1) * MXU_COLS
            cv = mid_ref[i_c + j] * mid_ref[i_v + j]
            taps_b.store(cvbuf, cv, c0)
            conv = cbw_ref[0:1, c0:c1] * taps_b.tap(cvbuf, 0, 0, rows, c0, c1, cv)
            for k in range(1, kb):
                conv = conv + cbw_ref[k:k + 1, c0:c1] * taps_b.tap(cvbuf, k, 0, rows, c0, c1, cv)
            s_ref[:, c0:c1] = (mid_ref[i_b + j] * conv).astype(BF16)

        for j in range(dm // MXU_COLS):
            c0, c1 = j * MXU_COLS, (j + 1) * MXU_COLS
            g_a = jax.nn.sigmoid(mid_ref[i_ga + j] + bg_ref[:, c0:c1])
            m_ref[:, c0:c1] = g_a * _dot(act_ref[...], wa_s[:, c0:c1])
        for j in range(dm // MXU_COLS):
            c0, c1 = j * MXU_COLS, (j + 1) * MXU_COLS
            g_b = jax.nn.sigmoid(mid_ref[i_gb + j] + bg_ref[:, dm + c0:dm + c1])
            mb_ref[:, c0:c1] = (m_ref[:, c0:c1] + g_b * _dot(s_ref[...], wb_s[:, c0:c1])).astype(BF16)
        o_ref[...] = x_ref[...] + _dot(mb_ref[...], wo_s[...])


def _mixer_scratch(rows, dm, ka, dc, kb, ds, n_mid):
    return [
        pltpu.VMEM((dm, 2 * dc), BF16),
        pltpu.VMEM((dm, n_mid * MXU_COLS), BF16),
        pltpu.VMEM((dc, dm), BF16),
        pltpu.VMEM((ds, dm), BF16),
        pltpu.VMEM((dm, dm), BF16),
        _TAPS_A(ka).scratch(rows, dc), _TAPS_3(kb).scratch(rows, ds),
        pltpu.VMEM((rows, dm), BF16),
        pltpu.VMEM((rows, dc), F32),
        pltpu.VMEM((n_mid, rows, MXU_COLS), F32),
        pltpu.VMEM((rows, dc), BF16),
        pltpu.VMEM((rows, ds), BF16),
        pltpu.VMEM((rows, dm), F32),
        pltpu.VMEM((rows, dm), BF16),
    ]


def _ffn_kernel(x_ref, p_ref, gf_ref, wup_ref, cfw_ref, cfb_ref, wdn_ref, gp_ref, wple_ref,
                wpg_ref, gfin_ref, o_ref,
                wup_s, wdn_s, wple_s, wpg_s, gbuf, h_ref, t0_ref, fbuf, pg_ref,
                *, tiles_per_seq, final_norm):
    rows, dm = x_ref.shape
    kf, dff = cfw_ref.shape
    taps = _TAPS_3(kf)
    step = pl.program_id(0)

    @pl.when(step < WEIGHT_STEPS)
    def _():
        _cast_rows(step, wup_ref, wup_s)
        _cast_rows(step, wdn_ref, wdn_s)
        _cast_rows(step, wple_ref, wple_s)
        _cast_rows(step, wpg_ref, wpg_s)

    @pl.when(step >= WEIGHT_STEPS)
    def _():
        first = ((step - WEIGHT_STEPS) % tiles_per_seq) == 0
        taps.carry(gbuf, rows, first)

        _rmsnorm_rows(x_ref, gf_ref, h_ref)
        for c0 in range(0, dff, MXU_COLS):
            c1 = min(c0 + MXU_COLS, dff)
            gate = _dot(h_ref[...], wup_s[:, c0:c1])
            t0_ref[:, 0:c1 - c0] = gate
            taps.store(gbuf, gate, c0)
            conv = jnp.broadcast_to(cfb_ref[:, c0:c1], (rows, c1 - c0))
            for k in range(kf):
                conv = conv + cfw_ref[k:k + 1, c0:c1] * taps.tap(gbuf, k, 0, rows, c0, c1,
                                                                t0_ref[:, 0:c1 - c0])
            val = _dot(h_ref[...], wup_s[:, dff + c0:dff + c1])
            fbuf[:, c0:c1] = (jax.nn.gelu(conv, approximate=True) * val).astype(BF16)
        o_ref[...] = x_ref[...] + _dot(fbuf[...], wdn_s[...])

        _rmsnorm_rows(o_ref, gp_ref, h_ref)
        pg_ref[...] = jax.nn.sigmoid(_dot(h_ref[...], wpg_s[...]))
        o_ref[...] = o_ref[...] + pg_ref[...] * _dot(p_ref[...].astype(BF16), wple_s[...])
        if final_norm:
            _rmsnorm_rows(o_ref, gfin_ref, o_ref)


def _ffn_scratch(rows, dm, kf, dff, dple):
    return [
        pltpu.VMEM((dm, 2 * dff), BF16),
        pltpu.VMEM((dff, dm), BF16),
        pltpu.VMEM((dple, dm), BF16),
        pltpu.VMEM((dm, dm), BF16),
        _TAPS_3(kf).scratch(rows, dff),
        pltpu.VMEM((rows, dm), BF16),
        pltpu.VMEM((rows, MXU_COLS), F32),
        pltpu.VMEM((rows, dff), BF16),
        pltpu.VMEM((rows, dm), F32),
    ]


def _layer_param(arr, layer):
    _, r, c = arr.shape
    return pl.BlockSpec((None, r, c), lambda i: (layer, 0, 0), pipeline_mode=pl.Buffered(1))


def _layer_weight(arr, layer):
    _, r, c = arr.shape
    assert r % (WEIGHT_STEPS * BF16_ROWS) == 0
    last = WEIGHT_STEPS - 1
    return pl.BlockSpec((None, r // WEIGHT_STEPS, c), lambda i: (layer, jnp.minimum(i, last), 0))


def _tile_index(i):
    return jnp.maximum(i - WEIGHT_STEPS, 0)


def _row_tiles(cols):
    return pl.BlockSpec((ROW_TILE, cols), lambda i: (_tile_index(i), 0))


def _compiler_params():
    return pltpu.CompilerParams(dimension_semantics=("arbitrary",),
                                vmem_limit_bytes=VMEM_LIMIT_BYTES)


def _mixer_call(x, seq, layer, g, w_in, b_gate, caw, cab, lng, lnb, wa, cbw, wb, wo):
    n, dm = x.shape
    ka, dc = caw.shape[1] // SUBLANES, caw.shape[2]
    kb, ds = cbw.shape[1:]
    assert ds % MXU_COLS == 0 and dm % MXU_COLS == 0 and dc % LANES == 0
    n_mid = (w_in.shape[2] - 2 * dc) // MXU_COLS
    spec = {id(w): _layer_weight(w, layer) for w in (w_in, wa, wb, wo)}
    operands = (g, w_in, b_gate, caw, cab, lng, lnb, wa, cbw, wb, wo)
    return pl.pallas_call(
        functools.partial(_mixer_kernel, tiles_per_seq=seq // ROW_TILE),
        grid=(WEIGHT_STEPS + n // ROW_TILE,),
        in_specs=[_row_tiles(dm)] + [spec.get(id(a)) or _layer_param(a, layer) for a in operands],
        out_specs=_row_tiles(dm),
        out_shape=jax.ShapeDtypeStruct((n, dm), x.dtype),
        scratch_shapes=_mixer_scratch(ROW_TILE, dm, ka, dc, kb, ds, n_mid),
        compiler_params=_compiler_params(),
        name="token_mixer",
    )(x, *operands)


def _ffn_call(x, p, seq, layer, gf, wup, cfw, cfb, wdn, gp, wple, wpg, gfin, final_norm):
    n, dm = x.shape
    kf, dff = cfw.shape[1:]
    dple = p.shape[2]
    spec = {id(w): _layer_weight(w, layer) for w in (wup, wdn, wple, wpg)}
    operands = (gf, wup, cfw, cfb, wdn, gp, wple, wpg)
    p_spec = pl.BlockSpec((None, ROW_TILE, dple), lambda i: (layer, _tile_index(i), 0))
    gfin_spec = pl.BlockSpec(gfin.shape, lambda i: (0, 0), pipeline_mode=pl.Buffered(1))
    return pl.pallas_call(
        functools.partial(_ffn_kernel, tiles_per_seq=seq // ROW_TILE, final_norm=final_norm),
        grid=(WEIGHT_STEPS + n // ROW_TILE,),
        in_specs=([_row_tiles(dm), p_spec]
                  + [spec.get(id(a)) or _layer_param(a, layer) for a in operands] + [gfin_spec]),
        out_specs=_row_tiles(dm),
        out_shape=jax.ShapeDtypeStruct((n, dm), x.dtype),
        scratch_shapes=_ffn_scratch(ROW_TILE, dm, kf, dff, dple),
        compiler_params=_compiler_params(),
        name="channel_mixer",
    )(x, p, *operands, gfin)


def kernel(x, p, g_mix, w_in, b_gate, conv_a_w, conv_a_b, ln_a_g, ln_a_b, w_a_out, conv_b_w, w_b_out, w_o, g_ffn, w_up, conv_f_w, conv_f_b, w_down, g_ple, w_ple, w_ple_gate, g_final):
    batch, seq, dm = x.shape
    depth = p.shape[0]
    assert seq % ROW_TILE == 0
    rows = lambda v: v.reshape(depth, 1, -1)
    caw = jnp.repeat(conv_a_w, SUBLANES, axis=1)
    xf = x.reshape(batch * seq, dm)
    pf = p.reshape(depth, batch * seq, -1)
    for i in range(depth):
        xf = _mixer_call(xf, seq, i, rows(g_mix), w_in, rows(b_gate), caw, rows(conv_a_b),
                         rows(ln_a_g), rows(ln_a_b), w_a_out, conv_b_w, w_b_out, w_o)
        xf = _ffn_call(xf, pf, seq, i, rows(g_ffn), w_up, conv_f_w, rows(conv_f_b), w_down,
                       rows(g_ple), w_ple, w_ple_gate, g_final.reshape(1, -1),
                       final_norm=(i == depth - 1))
    return xf.reshape(batch, seq, dm)
```

```python
import functools
import itertools

import jax
import jax.numpy as jnp
from jax import lax
from jax.experimental import pallas as pl
from jax.experimental.pallas import tpu as pltpu

EPS = 1e-6
SUBLANES = 8
BF16_ROWS = 16
LANES = 128
ROW_TILE = 512
WEIGHT_STEPS = 8
NORM_ROWS = 16
CONV_ROWS = 64
MXU_COLS = 512
FFN_COLS = 256
MID_PER_BLOCK = (2, 2, 2, 1)
VMEM_LIMIT_BYTES = 58 * 1024 * 1024

F32 = jnp.float32
BF16 = jnp.bfloat16


def _round_up(n, m):
    return (n + m - 1) // m * m


def _dot(a, b):
    return jnp.dot(a, b, preferred_element_type=F32)


def _rmsnorm(x, g):
    y = x * lax.rsqrt(jnp.mean(x * x, axis=-1, keepdims=True) + EPS)
    return y * g


def _rmsnorm_rows(src_ref, g_ref, dst_ref):
    g = g_ref[...]
    for r0 in range(0, src_ref.shape[0], NORM_ROWS):
        dst_ref[r0:r0 + NORM_ROWS, :] = _rmsnorm(src_ref[r0:r0 + NORM_ROWS, :], g).astype(dst_ref.dtype)


def _cast_rows(step, src_ref, dst_ref, src_cols=None, dst_col0=0):
    chunk = src_ref.shape[0]
    assert chunk % BF16_ROWS == 0
    rows = pl.ds(pl.multiple_of(step * chunk, chunk), chunk)
    c0, c1 = src_cols or (0, src_ref.shape[1])
    for c in range(c0, c1, MXU_COLS):
        w = min(MXU_COLS, c1 - c)
        dst_ref[rows, dst_col0 + c - c0:dst_col0 + c - c0 + w] = src_ref[:, c:c + w].astype(BF16)


class _Taps:
    def __init__(self, k_taps, cur_in_copies):
        self.halo = _round_up(k_taps - 1, SUBLANES)
        self.qr = [divmod(self.halo - (k_taps - 1) + k, SUBLANES) for k in range(k_taps)]
        self.cur_in_copies = cur_in_copies
        self.residues = sorted({r for q, r in self.qr if cur_in_copies or not self._is_cur(q, r)})

    def _is_cur(self, q, r):
        return (q * SUBLANES, r) == (self.halo, 0)

    def scratch(self, rows, cols):
        return pltpu.VMEM((len(self.residues), self.halo + rows, cols), F32)

    def carry(self, buf, rows, first):
        @pl.when(first)
        def _():
            buf[...] = jnp.zeros(buf.shape, buf.dtype)

        @pl.when(jnp.logical_not(first))
        def _():
            buf[:, 0:self.halo, :] = buf[:, rows:rows + self.halo, :]

    def store(self, buf, value, c0=0):
        rows, cols = value.shape
        for slot, r in enumerate(self.residues):
            buf[slot, self.halo - r:self.halo - r + rows, c0:c0 + cols] = value

    def tap(self, buf, k, r0, nrows, c0, c1, cur=None):
        q, r = self.qr[k]
        if self._is_cur(q, r) and not self.cur_in_copies:
            return cur
        start = r0 + q * SUBLANES
        return buf[self.residues.index(r), start:start + nrows, c0:c1]


_TAPS_A = functools.partial(_Taps, cur_in_copies=True)
_TAPS_3 = functools.partial(_Taps, cur_in_copies=False)


def _mixer_kernel(x_ref, g_ref, win_ref, bg_ref, caw_ref, cab_ref, lng_ref, lnb_ref,
                  wa_ref, cbw_ref, wb_ref, wo_ref, o_ref,
                  wglu_s, wmid_s, wa_s, wb_s, wo_s,
                  abuf, cvbuf, h_ref, conv_ref, mid_ref, act_ref, s_ref, m_ref, mb_ref,
                  *, tiles_per_seq):
    rows, dm = x_ref.shape
    ka, dc = caw_ref.shape[0] // SUBLANES, caw_ref.shape[1]
    kb, ds = cbw_ref.shape
    n_blocks = dc // LANES
    n_mid = mid_ref.shape[0]
    i_b, i_c, i_v = 0, ds // MXU_COLS, 2 * ds // MXU_COLS
    i_ga, i_gb = 3 * ds // MXU_COLS, (3 * ds + dm) // MXU_COLS
    taps_a, taps_b = _TAPS_A(ka), _TAPS_3(kb)
    step = pl.program_id(0)

    @pl.when(step < WEIGHT_STEPS)
    def _():
        for b in range(n_blocks):
            _cast_rows(step, win_ref, wglu_s, (b * LANES, (b + 1) * LANES), 2 * b * LANES)
            _cast_rows(step, win_ref, wglu_s, (dc + b * LANES, dc + (b + 1) * LANES),
                       (2 * b + 1) * LANES)
        _cast_rows(step, win_ref, wmid_s, (2 * dc, win_ref.shape[1]))
        _cast_rows(step, wa_ref, wa_s)
        _cast_rows(step, wb_ref, wb_s)
        _cast_rows(step, wo_ref, wo_s)

    @pl.when(step >= WEIGHT_STEPS)
    def _():
        first = ((step - WEIGHT_STEPS) % tiles_per_seq) == 0
        taps_a.carry(abuf, rows, first)
        taps_b.carry(cvbuf, rows, first)

        _rmsnorm_rows(x_ref, g_ref, h_ref)

        n_groups = CONV_ROWS // SUBLANES

        def glu_block(b):
            z = _dot(h_ref[...], wglu_s[:, 2 * b * LANES:2 * (b + 1) * LANES])
            taps_a.store(abuf, z[:, :LANES] * jax.nn.sigmoid(z[:, LANES:]), b * LANES)

        def conv_block(b):
            lanes = slice(b * LANES, (b + 1) * LANES)
            w = [caw_ref[k * SUBLANES:(k + 1) * SUBLANES, lanes] for k in range(ka)]
            bias = jnp.broadcast_to(cab_ref[:, lanes], (SUBLANES, LANES))
            for r0 in range(0, rows, CONV_ROWS):
                accs = [bias] * n_groups
                for slot, r in enumerate(taps_a.residues):
                    qk = [(q, k) for k, (q, rr) in enumerate(taps_a.qr) if rr == r]
                    for m in range(min(q for q, _ in qk), n_groups + max(q for q, _ in qk)):
                        win = abuf[slot, r0 + m * SUBLANES:r0 + (m + 1) * SUBLANES, lanes]
                        for q, k in qk:
                            if 0 <= m - q < n_groups:
                                accs[m - q] = accs[m - q] + w[k] * win
                conv_ref[r0:r0 + CONV_ROWS, lanes] = jnp.concatenate(accs, axis=0)

        def layernorm_swish():
            lng, lnb = lng_ref[...], lnb_ref[...]
            for r0 in range(0, rows, NORM_ROWS):
                acc = conv_ref[r0:r0 + NORM_ROWS, :]
                mu = jnp.mean(acc, axis=-1, keepdims=True)
                cen = acc - mu
                var = jnp.mean(cen * cen, axis=-1, keepdims=True)
                y = cen * lax.rsqrt(var + EPS) * lng + lnb
                act_ref[r0:r0 + NORM_ROWS, :] = (y * jax.nn.sigmoid(y)).astype(BF16)

        mid = iter(range(n_mid))

        def mid_dots(count):
            for i in itertools.islice(mid, count):
                mid_ref[i] = _dot(h_ref[...], wmid_s[:, i * MXU_COLS:(i + 1) * MXU_COLS])

        for b in range(n_blocks):
            glu_block(b)
            mid_dots(MID_PER_BLOCK[b])
            conv_block(b)
        mid_dots(n_mid)
        layernorm_swish()

        for j in range(ds // MXU_COLS):
            c0, c1 = j * MXU_COLS, (j + 1) * MXU_COLS
            cv = mid_ref[i_c + j] * mid_ref[i_v + j]
            taps_b.store(cvbuf, cv, c0)
            conv = cbw_ref[0:1, c0:c1] * taps_b.tap(cvbuf, 0, 0, rows, c0, c1, cv)
            for k in range(1, kb):
                conv = conv + cbw_ref[k:k + 1, c0:c1] * taps_b.tap(cvbuf, k, 0, rows, c0, c1, cv)
            s_ref[:, c0:c1] = (mid_ref[i_b + j] * conv).astype(BF16)

        for j in range(dm // MXU_COLS):
            c0, c1 = j * MXU_COLS, (j + 1) * MXU_COLS
            g_a = jax.nn.sigmoid(mid_ref[i_ga + j] + bg_ref[:, c0:c1])
            m_ref[:, c0:c1] = g_a * _dot(act_ref[...], wa_s[:, c0:c1])
        for j in range(dm // MXU_COLS):
            c0, c1 = j * MXU_COLS, (j + 1) * MXU_COLS
            g_b = jax.nn.sigmoid(mid_ref[i_gb + j] + bg_ref[:, dm + c0:dm + c1])
            mb_ref[:, c0:c1] = (m_ref[:, c0:c1] + g_b * _dot(s_ref[...], wb_s[:, c0:c1])).astype(BF16)
        o_ref[...] = x_ref[...] + _dot(mb_ref[...], wo_s[...])


def _mixer_scratch(rows, dm, ka, dc, kb, ds, n_mid):
    return [
        pltpu.VMEM((dm, 2 * dc), BF16),
        pltpu.VMEM((dm, n_mid * MXU_COLS), BF16),
        pltpu.VMEM((dc, dm), BF16),
        pltpu.VMEM((ds, dm), BF16),
        pltpu.VMEM((dm, dm), BF16),
        _TAPS_A(ka).scratch(rows, dc), _TAPS_3(kb).scratch(rows, ds),
        pltpu.VMEM((rows, dm), BF16),
        pltpu.VMEM((rows, dc), F32),
        pltpu.VMEM((n_mid, rows, MXU_COLS), F32),
        pltpu.VMEM((rows, dc), BF16),
        pltpu.VMEM((rows, ds), BF16),
        pltpu.VMEM((rows, dm), F32),
        pltpu.VMEM((rows, dm), BF16),
    ]


def _ffn_kernel(x_ref, p_ref, gf_ref, wup_ref, cfw_ref, cfb_ref, wdn_ref, gp_ref, wple_ref,
                wpg_ref, gfin_ref, o_ref,
                wup_s, wdn_s, wple_s, wpg_s, gbuf, h_ref, t0_ref, fbuf, pg_ref,
                *, tiles_per_seq, final_norm):
    rows, dm = x_ref.shape
    kf, dff = cfw_ref.shape
    taps = _TAPS_3(kf)
    step = pl.program_id(0)

    @pl.when(step < WEIGHT_STEPS)
    def _():
        _cast_rows(step, wup_ref, wup_s)
        _cast_rows(step, wdn_ref, wdn_s)
        _cast_rows(step, wple_ref, wple_s)
        _cast_rows(step, wpg_ref, wpg_s)

    @pl.when(step >= WEIGHT_STEPS)
    def _():
        first = ((step - WEIGHT_STEPS) % tiles_per_seq) == 0
        taps.carry(gbuf, rows, first)

        _rmsnorm_rows(x_ref, gf_ref, h_ref)
        for c0 in range(0, dff, FFN_COLS):
            c1 = min(c0 + FFN_COLS, dff)
            gate = _dot(h_ref[...], wup_s[:, c0:c1])
            t0_ref[:, 0:c1 - c0] = gate
            taps.store(gbuf, gate, c0)
            conv = jnp.broadcast_to(cfb_ref[:, c0:c1], (rows, c1 - c0))
            for k in range(kf):
                conv = conv + cfw_ref[k:k + 1, c0:c1] * taps.tap(gbuf, k, 0, rows, c0, c1,
                                                                t0_ref[:, 0:c1 - c0])
            val = _dot(h_ref[...], wup_s[:, dff + c0:dff + c1])
            fbuf[:, c0:c1] = (jax.nn.gelu(conv, approximate=True) * val).astype(BF16)
        o_ref[...] = x_ref[...] + _dot(fbuf[...], wdn_s[...])

        _rmsnorm_rows(o_ref, gp_ref, h_ref)
        pg_ref[...] = jax.nn.sigmoid(_dot(h_ref[...], wpg_s[...]))
        o_ref[...] = o_ref[...] + pg_ref[...] * _dot(p_ref[...].astype(BF16), wple_s[...])
        if final_norm:
            _rmsnorm_rows(o_ref, gfin_ref, o_ref)


def _ffn_scratch(rows, dm, kf, dff, dple):
    return [
        pltpu.VMEM((dm, 2 * dff), BF16),
        pltpu.VMEM((dff, dm), BF16),
        pltpu.VMEM((dple, dm), BF16),
        pltpu.VMEM((dm, dm), BF16),
        _TAPS_3(kf).scratch(rows, dff),
        pltpu.VMEM((rows, dm), BF16),
        pltpu.VMEM((rows, FFN_COLS), F32),
        pltpu.VMEM((rows, dff), BF16),
        pltpu.VMEM((rows, dm), F32),
    ]


def _layer_param(arr, layer):
    _, r, c = arr.shape
    return pl.BlockSpec((None, r, c), lambda i: (layer, 0, 0), pipeline_mode=pl.Buffered(1))


def _layer_weight(arr, layer):
    _, r, c = arr.shape
    assert r % (WEIGHT_STEPS * BF16_ROWS) == 0
    last = WEIGHT_STEPS - 1
    return pl.BlockSpec((None, r // WEIGHT_STEPS, c), lambda i: (layer, jnp.minimum(i, last), 0))


def _tile_index(i):
    return jnp.maximum(i - WEIGHT_STEPS, 0)


def _row_tiles(cols):
    return pl.BlockSpec((ROW_TILE, cols), lambda i: (_tile_index(i), 0))


def _compiler_params():
    return pltpu.CompilerParams(dimension_semantics=("arbitrary",),
                                vmem_limit_bytes=VMEM_LIMIT_BYTES)


def _mixer_call(x, seq, layer, g, w_in, b_gate, caw, cab, lng, lnb, wa, cbw, wb, wo):
    n, dm = x.shape
    ka, dc = caw.shape[1] // SUBLANES, caw.shape[2]
    kb, ds = cbw.shape[1:]
    assert ds % MXU_COLS == 0 and dm % MXU_COLS == 0 and dc % LANES == 0
    n_mid = (w_in.shape[2] - 2 * dc) // MXU_COLS
    spec = {id(w): _layer_weight(w, layer) for w in (w_in, wa, wb, wo)}
    operands = (g, w_in, b_gate, caw, cab, lng, lnb, wa, cbw, wb, wo)
    return pl.pallas_call(
        functools.partial(_mixer_kernel, tiles_per_seq=seq // ROW_TILE),
        grid=(WEIGHT_STEPS + n // ROW_TILE,),
        in_specs=[_row_tiles(dm)] + [spec.get(id(a)) or _layer_param(a, layer) for a in operands],
        out_specs=_row_tiles(dm),
        out_shape=jax.ShapeDtypeStruct((n, dm), x.dtype),
        scratch_shapes=_mixer_scratch(ROW_TILE, dm, ka, dc, kb, ds, n_mid),
        compiler_params=_compiler_params(),
        name="token_mixer",
    )(x, *operands)


def _ffn_call(x, p, seq, layer, gf, wup, cfw, cfb, wdn, gp, wple, wpg, gfin, final_norm):
    n, dm = x.shape
    kf, dff = cfw.shape[1:]
    dple = p.shape[2]
    spec = {id(w): _layer_weight(w, layer) for w in (wup, wdn, wple, wpg)}
    operands = (gf, wup, cfw, cfb, wdn, gp, wple, wpg)
    p_spec = pl.BlockSpec((None, ROW_TILE, dple), lambda i: (layer, _tile_index(i), 0))
    gfin_spec = pl.BlockSpec(gfin.shape, lambda i: (0, 0), pipeline_mode=pl.Buffered(1))
    return pl.pallas_call(
        functools.partial(_ffn_kernel, tiles_per_seq=seq // ROW_TILE, final_norm=final_norm),
        grid=(WEIGHT_STEPS + n // ROW_TILE,),
        in_specs=([_row_tiles(dm), p_spec]
                  + [spec.get(id(a)) or _layer_param(a, layer) for a in operands] + [gfin_spec]),
        out_specs=_row_tiles(dm),
        out_shape=jax.ShapeDtypeStruct((n, dm), x.dtype),
        scratch_shapes=_ffn_scratch(ROW_TILE, dm, kf, dff, dple),
        compiler_params=_compiler_params(),
        name="channel_mixer",
    )(x, p, *operands, gfin)


def kernel(x, p, g_mix, w_in, b_gate, conv_a_w, conv_a_b, ln_a_g, ln_a_b, w_a_out, conv_b_w, w_b_out, w_o, g_ffn, w_up, conv_f_w, conv_f_b, w_down, g_ple, w_ple, w_ple_gate, g_final):
    batch, seq, dm = x.shape
    depth = p.shape[0]
    assert seq % ROW_TILE == 0
    rows = lambda v: v.reshape(depth, 1, -1)
    caw = jnp.repeat(conv_a_w, SUBLANES, axis=1)
    xf = x.reshape(batch * seq, dm)
    pf = p.reshape(depth, batch * seq, -1)
    for i in range(depth):
        xf = _mixer_call(xf, seq, i, rows(g_mix), w_in, rows(b_gate), caw, rows(conv_a_b),
                         rows(ln_a_g), rows(ln_a_b), w_a_out, conv_b_w, w_b_out, w_o)
        xf = _ffn_call(xf, pf, seq, i, rows(g_ffn), w_up, conv_f_w, rows(conv_f_b), w_down,
                       rows(g_ple), w_ple, w_ple_gate, g_final.reshape(1, -1),
                       final_norm=(i == depth - 1))
    return xf.reshape(batch, seq, dm)
```

```python
import functools
import itertools

import jax
import jax.numpy as jnp
from jax import lax
from jax.experimental import pallas as pl
from jax.experimental.pallas import tpu as pltpu

EPS = 1e-6
SUBLANES = 8
BF16_ROWS = 16
LANES = 128
ROW_TILE = 512
WEIGHT_STEPS = 8
NORM_ROWS = 16
CONV_ROWS = 64
MXU_COLS = 512
FFN_COLS = 256
MID_PER_BLOCK = (3, 1, 2, 1)
SHORT_CONV_BLOCK = 0
VMEM_LIMIT_BYTES = 58 * 1024 * 1024

F32 = jnp.float32
BF16 = jnp.bfloat16


def _round_up(n, m):
    return (n + m - 1) // m * m


def _dot(a, b):
    return jnp.dot(a, b, preferred_element_type=F32)


def _rmsnorm(x, g):
    y = x * lax.rsqrt(jnp.mean(x * x, axis=-1, keepdims=True) + EPS)
    return y * g


def _rmsnorm_rows(src_ref, g_ref, dst_ref):
    g = g_ref[...]
    for r0 in range(0, src_ref.shape[0], NORM_ROWS):
        dst_ref[r0:r0 + NORM_ROWS, :] = _rmsnorm(src_ref[r0:r0 + NORM_ROWS, :], g).astype(dst_ref.dtype)


def _cast_rows(step, src_ref, dst_ref, src_cols=None, dst_col0=0):
    chunk = src_ref.shape[0]
    assert chunk % BF16_ROWS == 0
    rows = pl.ds(pl.multiple_of(step * chunk, chunk), chunk)
    c0, c1 = src_cols or (0, src_ref.shape[1])
    for c in range(c0, c1, MXU_COLS):
        w = min(MXU_COLS, c1 - c)
        dst_ref[rows, dst_col0 + c - c0:dst_col0 + c - c0 + w] = src_ref[:, c:c + w].astype(BF16)


class _Taps:
    def __init__(self, k_taps, cur_in_copies):
        self.halo = _round_up(k_taps - 1, SUBLANES)
        self.qr = [divmod(self.halo - (k_taps - 1) + k, SUBLANES) for k in range(k_taps)]
        self.cur_in_copies = cur_in_copies
        self.residues = sorted({r for q, r in self.qr if cur_in_copies or not self._is_cur(q, r)})

    def _is_cur(self, q, r):
        return (q * SUBLANES, r) == (self.halo, 0)

    def scratch(self, rows, cols):
        return pltpu.VMEM((len(self.residues), self.halo + rows, cols), F32)

    def carry(self, buf, rows, first):
        @pl.when(first)
        def _():
            buf[...] = jnp.zeros(buf.shape, buf.dtype)

        @pl.when(jnp.logical_not(first))
        def _():
            buf[:, 0:self.halo, :] = buf[:, rows:rows + self.halo, :]

    def store(self, buf, value, c0=0):
        rows, cols = value.shape
        for slot, r in enumerate(self.residues):
            buf[slot, self.halo - r:self.halo - r + rows, c0:c0 + cols] = value

    def tap(self, buf, k, r0, nrows, c0, c1, cur=None):
        q, r = self.qr[k]
        if self._is_cur(q, r) and not self.cur_in_copies:
            return cur
        start = r0 + q * SUBLANES
        return buf[self.residues.index(r), start:start + nrows, c0:c1]


_TAPS_A = functools.partial(_Taps, cur_in_copies=True)
_TAPS_3 = functools.partial(_Taps, cur_in_copies=False)


def _mixer_kernel(x_ref, g_ref, win_ref, bg_ref, caw_ref, cab_ref, lng_ref, lnb_ref,
                  wa_ref, cbw_ref, wb_ref, wo_ref, o_ref,
                  wglu_s, wmid_s, wa_s, wb_s, wo_s,
                  abuf, cvbuf, h_ref, conv_ref, mid_ref, act_ref, s_ref, m_ref, mb_ref,
                  *, tiles_per_seq):
    rows, dm = x_ref.shape
    ka, dc = caw_ref.shape[0] // SUBLANES, caw_ref.shape[1]
    kb, ds = cbw_ref.shape
    n_blocks = dc // LANES
    n_mid = mid_ref.shape[0]
    i_b, i_c, i_v = 0, ds // MXU_COLS, 2 * ds // MXU_COLS
    i_ga, i_gb = 3 * ds // MXU_COLS, (3 * ds + dm) // MXU_COLS
    taps_a, taps_b = _TAPS_A(ka), _TAPS_3(kb)
    step = pl.program_id(0)

    @pl.when(step < WEIGHT_STEPS)
    def _():
        for b in range(n_blocks):
            _cast_rows(step, win_ref, wglu_s, (b * LANES, (b + 1) * LANES), 2 * b * LANES)
            _cast_rows(step, win_ref, wglu_s, (dc + b * LANES, dc + (b + 1) * LANES),
                       (2 * b + 1) * LANES)
        _cast_rows(step, win_ref, wmid_s, (2 * dc, win_ref.shape[1]))
        _cast_rows(step, wa_ref, wa_s)
        _cast_rows(step, wb_ref, wb_s)
        _cast_rows(step, wo_ref, wo_s)

    @pl.when(step >= WEIGHT_STEPS)
    def _():
        first = ((step - WEIGHT_STEPS) % tiles_per_seq) == 0
        taps_a.carry(abuf, rows, first)
        taps_b.carry(cvbuf, rows, first)

        _rmsnorm_rows(x_ref, g_ref, h_ref)

        n_groups = CONV_ROWS // SUBLANES

        def glu_block(b):
            z = _dot(h_ref[...], wglu_s[:, 2 * b * LANES:2 * (b + 1) * LANES])
            taps_a.store(abuf, z[:, :LANES] * jax.nn.sigmoid(z[:, LANES:]), b * LANES)

        def conv_block(b):
            lanes = slice(b * LANES, (b + 1) * LANES)
            w = [caw_ref[k * SUBLANES:(k + 1) * SUBLANES, lanes] for k in range(ka)]
            bias = jnp.broadcast_to(cab_ref[:, lanes], (SUBLANES, LANES))
            for r0 in range(0, rows, CONV_ROWS):
                accs = [bias] * n_groups
                for slot, r in enumerate(taps_a.residues):
                    qk = [(q, k) for k, (q, rr) in enumerate(taps_a.qr) if rr == r]
                    for m in range(min(q for q, _ in qk), n_groups + max(q for q, _ in qk)):
                        win = abuf[slot, r0 + m * SUBLANES:r0 + (m + 1) * SUBLANES, lanes]
                        for q, k in qk:
                            if 0 <= m - q < n_groups:
                                accs[m - q] = accs[m - q] + w[k] * win
                conv_ref[r0:r0 + CONV_ROWS, lanes] = jnp.concatenate(accs, axis=0)

        def layernorm_swish():
            lng, lnb = lng_ref[...], lnb_ref[...]
            for r0 in range(0, rows, NORM_ROWS):
                acc = conv_ref[r0:r0 + NORM_ROWS, :]
                mu = jnp.mean(acc, axis=-1, keepdims=True)
                cen = acc - mu
                var = jnp.mean(cen * cen, axis=-1, keepdims=True)
                y = cen * lax.rsqrt(var + EPS) * lng + lnb
                act_ref[r0:r0 + NORM_ROWS, :] = (y * jax.nn.sigmoid(y)).astype(BF16)

        mid = iter(range(n_mid))

        def mid_dots(count):
            for i in itertools.islice(mid, count):
                mid_ref[i] = _dot(h_ref[...], wmid_s[:, i * MXU_COLS:(i + 1) * MXU_COLS])

        def short_conv():
            for j in range(ds // MXU_COLS):
                c0, c1 = j * MXU_COLS, (j + 1) * MXU_COLS
                cv = mid_ref[i_c + j] * mid_ref[i_v + j]
                taps_b.store(cvbuf, cv, c0)
                conv = cbw_ref[0:1, c0:c1] * taps_b.tap(cvbuf, 0, 0, rows, c0, c1, cv)
                for k in range(1, kb):
                    conv = conv + cbw_ref[k:k + 1, c0:c1] * taps_b.tap(cvbuf, k, 0, rows, c0, c1, cv)
                s_ref[:, c0:c1] = (mid_ref[i_b + j] * conv).astype(BF16)

        def project_b():
            for j in range(dm // MXU_COLS):
                c0, c1 = j * MXU_COLS, (j + 1) * MXU_COLS
                m_ref[:, c0:c1] = _dot(s_ref[...], wb_s[:, c0:c1])

        for b in range(n_blocks):
            glu_block(b)
            mid_dots(MID_PER_BLOCK[b])
            if b == SHORT_CONV_BLOCK:
                short_conv()
            if b == n_blocks - 1:
                project_b()
            conv_block(b)
        mid_dots(n_mid)
        layernorm_swish()

        for j in range(dm // MXU_COLS):
            c0, c1 = j * MXU_COLS, (j + 1) * MXU_COLS
            g_a = jax.nn.sigmoid(mid_ref[i_ga + j] + bg_ref[:, c0:c1])
            g_b = jax.nn.sigmoid(mid_ref[i_gb + j] + bg_ref[:, dm + c0:dm + c1])
            y_a = _dot(act_ref[...], wa_s[:, c0:c1])
            mb_ref[:, c0:c1] = (g_a * y_a + g_b * m_ref[:, c0:c1]).astype(BF16)
        o_ref[...] = x_ref[...] + _dot(mb_ref[...], wo_s[...])


def _mixer_scratch(rows, dm, ka, dc, kb, ds, n_mid):
    return [
        pltpu.VMEM((dm, 2 * dc), BF16),
        pltpu.VMEM((dm, n_mid * MXU_COLS), BF16),
        pltpu.VMEM((dc, dm), BF16),
        pltpu.VMEM((ds, dm), BF16),
        pltpu.VMEM((dm, dm), BF16),
        _TAPS_A(ka).scratch(rows, dc), _TAPS_3(kb).scratch(rows, ds),
        pltpu.VMEM((rows, dm), BF16),
        pltpu.VMEM((rows, dc), F32),
        pltpu.VMEM((n_mid, rows, MXU_COLS), F32),
        pltpu.VMEM((rows, dc), BF16),
        pltpu.VMEM((rows, ds), BF16),
        pltpu.VMEM((rows, dm), F32),
        pltpu.VMEM((rows, dm), BF16),
    ]


def _ffn_kernel(x_ref, p_ref, gf_ref, wup_ref, cfw_ref, cfb_ref, wdn_ref, gp_ref, wple_ref,
                wpg_ref, gfin_ref, o_ref,
                wup_s, wdn_s, wple_s, wpg_s, gbuf, h_ref, t0_ref, fbuf, pb_ref,
                *, tiles_per_seq, final_norm):
    rows, dm = x_ref.shape
    kf, dff = cfw_ref.shape
    taps = _TAPS_3(kf)
    step = pl.program_id(0)

    @pl.when(step < WEIGHT_STEPS)
    def _():
        _cast_rows(step, wup_ref, wup_s)
        _cast_rows(step, wdn_ref, wdn_s)
        _cast_rows(step, wple_ref, wple_s)
        _cast_rows(step, wpg_ref, wpg_s)

    @pl.when(step >= WEIGHT_STEPS)
    def _():
        first = ((step - WEIGHT_STEPS) % tiles_per_seq) == 0
        taps.carry(gbuf, rows, first)

        _rmsnorm_rows(x_ref, gf_ref, h_ref)
        for c0 in range(0, dff, FFN_COLS):
            c1 = min(c0 + FFN_COLS, dff)
            gate = _dot(h_ref[...], wup_s[:, c0:c1])
            t0_ref[:, 0:c1 - c0] = gate
            taps.store(gbuf, gate, c0)
            conv = jnp.broadcast_to(cfb_ref[:, c0:c1], (rows, c1 - c0))
            for k in range(kf):
                conv = conv + cfw_ref[k:k + 1, c0:c1] * taps.tap(gbuf, k, 0, rows, c0, c1,
                                                                t0_ref[:, 0:c1 - c0])
            val = _dot(h_ref[...], wup_s[:, dff + c0:dff + c1])
            fbuf[:, c0:c1] = (jax.nn.gelu(conv, approximate=True) * val).astype(BF16)
        o_ref[...] = x_ref[...] + _dot(fbuf[...], wdn_s[...])

        _rmsnorm_rows(o_ref, gp_ref, h_ref)
        pb_ref[...] = p_ref[...].astype(BF16)
        for c0 in range(0, dm, FFN_COLS):
            c1 = c0 + FFN_COLS
            gate = jax.nn.sigmoid(_dot(h_ref[...], wpg_s[:, c0:c1]))
            o_ref[:, c0:c1] = o_ref[:, c0:c1] + gate * _dot(pb_ref[...], wple_s[:, c0:c1])
        if final_norm:
            _rmsnorm_rows(o_ref, gfin_ref, o_ref)


def _ffn_scratch(rows, dm, kf, dff, dple):
    return [
        pltpu.VMEM((dm, 2 * dff), BF16),
        pltpu.VMEM((dff, dm), BF16),
        pltpu.VMEM((dple, dm), BF16),
        pltpu.VMEM((dm, dm), BF16),
        _TAPS_3(kf).scratch(rows, dff),
        pltpu.VMEM((rows, dm), BF16),
        pltpu.VMEM((rows, FFN_COLS), F32),
        pltpu.VMEM((rows, dff), BF16),
        pltpu.VMEM((rows, dple), BF16),
    ]


def _layer_param(arr, layer):
    _, r, c = arr.shape
    return pl.BlockSpec((None, r, c), lambda i: (layer, 0, 0), pipeline_mode=pl.Buffered(1))


def _layer_weight(arr, layer):
    _, r, c = arr.shape
    assert r % (WEIGHT_STEPS * BF16_ROWS) == 0
    last = WEIGHT_STEPS - 1
    return pl.BlockSpec((None, r // WEIGHT_STEPS, c), lambda i: (layer, jnp.minimum(i, last), 0))


def _tile_index(i):
    return jnp.maximum(i - WEIGHT_STEPS, 0)


def _row_tiles(cols):
    return pl.BlockSpec((ROW_TILE, cols), lambda i: (_tile_index(i), 0))


def _compiler_params():
    return pltpu.CompilerParams(dimension_semantics=("arbitrary",),
                                vmem_limit_bytes=VMEM_LIMIT_BYTES)


def _mixer_call(x, seq, layer, g, w_in, b_gate, caw, cab, lng, lnb, wa, cbw, wb, wo):
    n, dm = x.shape
    ka, dc = caw.shape[1] // SUBLANES, caw.shape[2]
    kb, ds = cbw.shape[1:]
    assert ds % MXU_COLS == 0 and dm % MXU_COLS == 0 and dc % LANES == 0
    n_mid = (w_in.shape[2] - 2 * dc) // MXU_COLS
    spec = {id(w): _layer_weight(w, layer) for w in (w_in, wa, wb, wo)}
    operands = (g, w_in, b_gate, caw, cab, lng, lnb, wa, cbw, wb, wo)
    return pl.pallas_call(
        functools.partial(_mixer_kernel, tiles_per_seq=seq // ROW_TILE),
        grid=(WEIGHT_STEPS + n // ROW_TILE,),
        in_specs=[_row_tiles(dm)] + [spec.get(id(a)) or _layer_param(a, layer) for a in operands],
        out_specs=_row_tiles(dm),
        out_shape=jax.ShapeDtypeStruct((n, dm), x.dtype),
        scratch_shapes=_mixer_scratch(ROW_TILE, dm, ka, dc, kb, ds, n_mid),
        compiler_params=_compiler_params(),
        name="token_mixer",
    )(x, *operands)


def _ffn_call(x, p, seq, layer, gf, wup, cfw, cfb, wdn, gp, wple, wpg, gfin, final_norm):
    n, dm = x.shape
    kf, dff = cfw.shape[1:]
    dple = p.shape[2]
    spec = {id(w): _layer_weight(w, layer) for w in (wup, wdn, wple, wpg)}
    operands = (gf, wup, cfw, cfb, wdn, gp, wple, wpg)
    p_spec = pl.BlockSpec((None, ROW_TILE, dple), lambda i: (layer, _tile_index(i), 0))
    gfin_spec = pl.BlockSpec(gfin.shape, lambda i: (0, 0), pipeline_mode=pl.Buffered(1))
    return pl.pallas_call(
        functools.partial(_ffn_kernel, tiles_per_seq=seq // ROW_TILE, final_norm=final_norm),
        grid=(WEIGHT_STEPS + n // ROW_TILE,),
        in_specs=([_row_tiles(dm), p_spec]
                  + [spec.get(id(a)) or _layer_param(a, layer) for a in operands] + [gfin_spec]),
        out_specs=_row_tiles(dm),
        out_shape=jax.ShapeDtypeStruct((n, dm), x.dtype),
        scratch_shapes=_ffn_scratch(ROW_TILE, dm, kf, dff, dple),
        compiler_params=_compiler_params(),
        name="channel_mixer",
    )(x, p, *operands, gfin)


def kernel(x, p, g_mix, w_in, b_gate, conv_a_w, conv_a_b, ln_a_g, ln_a_b, w_a_out, conv_b_w, w_b_out, w_o, g_ffn, w_up, conv_f_w, conv_f_b, w_down, g_ple, w_ple, w_ple_gate, g_final):
    batch, seq, dm = x.shape
    depth = p.shape[0]
    assert seq % ROW_TILE == 0
    rows = lambda v: v.reshape(depth, 1, -1)
    caw = jnp.repeat(conv_a_w, SUBLANES, axis=1)
    xf = x.reshape(batch * seq, dm)
    pf = p.reshape(depth, batch * seq, -1)
    for i in range(depth):
        xf = _mixer_call(xf, seq, i, rows(g_mix), w_in, rows(b_gate), caw, rows(conv_a_b),
                         rows(ln_a_g), rows(ln_a_b), w_a_out, conv_b_w, w_b_out, w_o)
        xf = _ffn_call(xf, pf, seq, i, rows(g_ffn), w_up, conv_f_w, rows(conv_f_b), w_down,
                       rows(g_ple), w_ple, w_ple_gate, g_final.reshape(1, -1),
                       final_norm=(i == depth - 1))
    return xf.reshape(batch, seq, dm)
```

```python
import functools
import itertools

import jax
import jax.numpy as jnp
from jax import lax
from jax.experimental import pallas as pl
from jax.experimental.pallas import tpu as pltpu

EPS = 1e-6
SUBLANES = 8
BF16_ROWS = 16
LANES = 128
ROW_TILE = 512
WEIGHT_STEPS = 8
NORM_ROWS = 16
CONV_ROWS = 64
MXU_COLS = 512
FFN_COLS = 256
MID_PER_BLOCK = (3, 1, 2, 1)
SHORT_CONV_BLOCK = 0
VMEM_LIMIT_BYTES = 58 * 1024 * 1024

F32 = jnp.float32
BF16 = jnp.bfloat16


def _round_up(n, m):
    return (n + m - 1) // m * m


def _dot(a, b):
    return jnp.dot(a, b, preferred_element_type=F32)


def _rmsnorm(x, g):
    y = x * lax.rsqrt(jnp.mean(x * x, axis=-1, keepdims=True) + EPS)
    return y * g


def _rmsnorm_rows(src_ref, g_ref, dst_ref):
    g = g_ref[...]
    for r0 in range(0, src_ref.shape[0], NORM_ROWS):
        dst_ref[r0:r0 + NORM_ROWS, :] = _rmsnorm(src_ref[r0:r0 + NORM_ROWS, :], g).astype(dst_ref.dtype)


def _cast_rows(step, src_ref, dst_ref, src_cols=None, dst_col0=0):
    chunk = src_ref.shape[0]
    assert chunk % BF16_ROWS == 0
    rows = pl.ds(pl.multiple_of(step * chunk, chunk), chunk)
    c0, c1 = src_cols or (0, src_ref.shape[1])
    for c in range(c0, c1, MXU_COLS):
        w = min(MXU_COLS, c1 - c)
        dst_ref[rows, dst_col0 + c - c0:dst_col0 + c - c0 + w] = src_ref[:, c:c + w].astype(BF16)


class _Taps:
    def __init__(self, k_taps, cur_in_copies):
        self.halo = _round_up(k_taps - 1, SUBLANES)
        self.qr = [divmod(self.halo - (k_taps - 1) + k, SUBLANES) for k in range(k_taps)]
        self.cur_in_copies = cur_in_copies
        self.residues = sorted({r for q, r in self.qr if cur_in_copies or not self._is_cur(q, r)})

    def _is_cur(self, q, r):
        return (q * SUBLANES, r) == (self.halo, 0)

    def scratch(self, rows, cols):
        return pltpu.VMEM((len(self.residues), self.halo + rows, cols), F32)

    def carry(self, buf, rows, first):
        @pl.when(first)
        def _():
            buf[...] = jnp.zeros(buf.shape, buf.dtype)

        @pl.when(jnp.logical_not(first))
        def _():
            buf[:, 0:self.halo, :] = buf[:, rows:rows + self.halo, :]

    def store(self, buf, value, c0=0):
        rows, cols = value.shape
        for slot, r in enumerate(self.residues):
            buf[slot, self.halo - r:self.halo - r + rows, c0:c0 + cols] = value

    def tap(self, buf, k, r0, nrows, c0, c1, cur=None):
        q, r = self.qr[k]
        if self._is_cur(q, r) and not self.cur_in_copies:
            return cur
        start = r0 + q * SUBLANES
        return buf[self.residues.index(r), start:start + nrows, c0:c1]


_TAPS_A = functools.partial(_Taps, cur_in_copies=True)
_TAPS_3 = functools.partial(_Taps, cur_in_copies=False)


def _mixer_kernel(x_ref, g_ref, win_ref, bg_ref, caw_ref, cab_ref, lng_ref, lnb_ref,
                  wa_ref, cbw_ref, wb_ref, wo_ref, o_ref,
                  wglu_s, wmid_s, wa_s, wb_s, wo_s,
                  abuf, cvbuf, h_ref, conv_ref, mid_ref, act_ref, s_ref, m_ref, mb_ref,
                  *, tiles_per_seq, layer):
    g_ref, bg_ref, cab_ref, lng_ref, lnb_ref = (
        r.at[layer:layer + 1] for r in (g_ref, bg_ref, cab_ref, lng_ref, lnb_ref))
    rows, dm = x_ref.shape
    ka, dc = caw_ref.shape[0] // SUBLANES, caw_ref.shape[1]
    kb, ds = cbw_ref.shape
    n_blocks = dc // LANES
    n_mid = mid_ref.shape[0]
    i_b, i_c, i_v = 0, ds // MXU_COLS, 2 * ds // MXU_COLS
    i_ga, i_gb = 3 * ds // MXU_COLS, (3 * ds + dm) // MXU_COLS
    taps_a, taps_b = _TAPS_A(ka), _TAPS_3(kb)
    step = pl.program_id(0)

    @pl.when(step < WEIGHT_STEPS)
    def _():
        for b in range(n_blocks):
            _cast_rows(step, win_ref, wglu_s, (b * LANES, (b + 1) * LANES), 2 * b * LANES)
            _cast_rows(step, win_ref, wglu_s, (dc + b * LANES, dc + (b + 1) * LANES),
                       (2 * b + 1) * LANES)
        _cast_rows(step, win_ref, wmid_s, (2 * dc, win_ref.shape[1]))
        _cast_rows(step, wa_ref, wa_s)
        _cast_rows(step, wb_ref, wb_s)
        _cast_rows(step, wo_ref, wo_s)

    @pl.when(step >= WEIGHT_STEPS)
    def _():
        first = ((step - WEIGHT_STEPS) % tiles_per_seq) == 0
        taps_a.carry(abuf, rows, first)
        taps_b.carry(cvbuf, rows, first)

        _rmsnorm_rows(x_ref, g_ref, h_ref)

        n_groups = CONV_ROWS // SUBLANES

        def glu_block(b):
            z = _dot(h_ref[...], wglu_s[:, 2 * b * LANES:2 * (b + 1) * LANES])
            taps_a.store(abuf, z[:, :LANES] * jax.nn.sigmoid(z[:, LANES:]), b * LANES)

        def conv_block(b):
            lanes = slice(b * LANES, (b + 1) * LANES)
            w = [caw_ref[k * SUBLANES:(k + 1) * SUBLANES, lanes] for k in range(ka)]
            bias = jnp.broadcast_to(cab_ref[:, lanes], (SUBLANES, LANES))
            for r0 in range(0, rows, CONV_ROWS):
                accs = [bias] * n_groups
                for slot, r in enumerate(taps_a.residues):
                    qk = [(q, k) for k, (q, rr) in enumerate(taps_a.qr) if rr == r]
                    for m in range(min(q for q, _ in qk), n_groups + max(q for q, _ in qk)):
                        win = abuf[slot, r0 + m * SUBLANES:r0 + (m + 1) * SUBLANES, lanes]
                        for q, k in qk:
                            if 0 <= m - q < n_groups:
                                accs[m - q] = accs[m - q] + w[k] * win
                conv_ref[r0:r0 + CONV_ROWS, lanes] = jnp.concatenate(accs, axis=0)

        def layernorm_swish():
            lng, lnb = lng_ref[...], lnb_ref[...]
            for r0 in range(0, rows, NORM_ROWS):
                acc = conv_ref[r0:r0 + NORM_ROWS, :]
                mu = jnp.mean(acc, axis=-1, keepdims=True)
                cen = acc - mu
                var = jnp.mean(cen * cen, axis=-1, keepdims=True)
                y = cen * lax.rsqrt(var + EPS) * lng + lnb
                act_ref[r0:r0 + NORM_ROWS, :] = (y * jax.nn.sigmoid(y)).astype(BF16)

        mid = iter(range(n_mid))

        def mid_dots(count):
            for i in itertools.islice(mid, count):
                mid_ref[i] = _dot(h_ref[...], wmid_s[:, i * MXU_COLS:(i + 1) * MXU_COLS])

        def short_conv():
            for j in range(ds // MXU_COLS):
                c0, c1 = j * MXU_COLS, (j + 1) * MXU_COLS
                cv = mid_ref[i_c + j] * mid_ref[i_v + j]
                taps_b.store(cvbuf, cv, c0)
                conv = cbw_ref[0:1, c0:c1] * taps_b.tap(cvbuf, 0, 0, rows, c0, c1, cv)
                for k in range(1, kb):
                    conv = conv + cbw_ref[k:k + 1, c0:c1] * taps_b.tap(cvbuf, k, 0, rows, c0, c1, cv)
                s_ref[:, c0:c1] = (mid_ref[i_b + j] * conv).astype(BF16)

        def project_b():
            for j in range(dm // MXU_COLS):
                c0, c1 = j * MXU_COLS, (j + 1) * MXU_COLS
                m_ref[:, c0:c1] = _dot(s_ref[...], wb_s[:, c0:c1])

        for b in range(n_blocks):
            glu_block(b)
            mid_dots(MID_PER_BLOCK[b])
            if b == SHORT_CONV_BLOCK:
                short_conv()
            if b == n_blocks - 1:
                project_b()
            conv_block(b)
        mid_dots(n_mid)
        layernorm_swish()

        for j in range(dm // MXU_COLS):
            c0, c1 = j * MXU_COLS, (j + 1) * MXU_COLS
            g_a = jax.nn.sigmoid(mid_ref[i_ga + j] + bg_ref[:, c0:c1])
            g_b = jax.nn.sigmoid(mid_ref[i_gb + j] + bg_ref[:, dm + c0:dm + c1])
            y_a = _dot(act_ref[...], wa_s[:, c0:c1])
            mb_ref[:, c0:c1] = (g_a * y_a + g_b * m_ref[:, c0:c1]).astype(BF16)
        o_ref[...] = x_ref[...] + _dot(mb_ref[...], wo_s[...])


def _mixer_scratch(rows, dm, ka, dc, kb, ds, n_mid):
    return [
        pltpu.VMEM((dm, 2 * dc), BF16),
        pltpu.VMEM((dm, n_mid * MXU_COLS), BF16),
        pltpu.VMEM((dc, dm), BF16),
        pltpu.VMEM((ds, dm), BF16),
        pltpu.VMEM((dm, dm), BF16),
        _TAPS_A(ka).scratch(rows, dc), _TAPS_3(kb).scratch(rows, ds),
        pltpu.VMEM((rows, dm), BF16),
        pltpu.VMEM((rows, dc), F32),
        pltpu.VMEM((n_mid, rows, MXU_COLS), F32),
        pltpu.VMEM((rows, dc), BF16),
        pltpu.VMEM((rows, ds), BF16),
        pltpu.VMEM((rows, dm), F32),
        pltpu.VMEM((rows, dm), BF16),
    ]


def _ffn_kernel(x_ref, p_ref, gf_ref, wup_ref, cfw_ref, cfb_ref, wdn_ref, gp_ref, wple_ref,
                wpg_ref, gfin_ref, o_ref,
                wup_s, wdn_s, wple_s, wpg_s, gbuf, h_ref, t0_ref, fbuf, pb_ref,
                *, tiles_per_seq, layer, final_norm):
    gf_ref, cfb_ref, gp_ref = (r.at[layer:layer + 1] for r in (gf_ref, cfb_ref, gp_ref))
    rows, dm = x_ref.shape
    kf, dff = cfw_ref.shape
    taps = _TAPS_3(kf)
    step = pl.program_id(0)

    @pl.when(step < WEIGHT_STEPS)
    def _():
        _cast_rows(step, wup_ref, wup_s)
        _cast_rows(step, wdn_ref, wdn_s)
        _cast_rows(step, wple_ref, wple_s)
        _cast_rows(step, wpg_ref, wpg_s)

    @pl.when(step >= WEIGHT_STEPS)
    def _():
        first = ((step - WEIGHT_STEPS) % tiles_per_seq) == 0
        taps.carry(gbuf, rows, first)

        _rmsnorm_rows(x_ref, gf_ref, h_ref)
        for c0 in range(0, dff, FFN_COLS):
            c1 = min(c0 + FFN_COLS, dff)
            gate = _dot(h_ref[...], wup_s[:, c0:c1])
            t0_ref[:, 0:c1 - c0] = gate
            taps.store(gbuf, gate, c0)
            conv = jnp.broadcast_to(cfb_ref[:, c0:c1], (rows, c1 - c0))
            for k in range(kf):
                conv = conv + cfw_ref[k:k + 1, c0:c1] * taps.tap(gbuf, k, 0, rows, c0, c1,
                                                                t0_ref[:, 0:c1 - c0])
            val = _dot(h_ref[...], wup_s[:, dff + c0:dff + c1])
            fbuf[:, c0:c1] = (jax.nn.gelu(conv, approximate=True) * val).astype(BF16)
        o_ref[...] = x_ref[...] + _dot(fbuf[...], wdn_s[...])

        _rmsnorm_rows(o_ref, gp_ref, h_ref)
        pb_ref[...] = p_ref[...].astype(BF16)
        for c0 in range(0, dm, FFN_COLS):
            c1 = c0 + FFN_COLS
            gate = jax.nn.sigmoid(_dot(h_ref[...], wpg_s[:, c0:c1]))
            o_ref[:, c0:c1] = o_ref[:, c0:c1] + gate * _dot(pb_ref[...], wple_s[:, c0:c1])
        if final_norm:
            _rmsnorm_rows(o_ref, gfin_ref, o_ref)


def _ffn_scratch(rows, dm, kf, dff, dple):
    return [
        pltpu.VMEM((dm, 2 * dff), BF16),
        pltpu.VMEM((dff, dm), BF16),
        pltpu.VMEM((dple, dm), BF16),
        pltpu.VMEM((dm, dm), BF16),
        _TAPS_3(kf).scratch(rows, dff),
        pltpu.VMEM((rows, dm), BF16),
        pltpu.VMEM((rows, FFN_COLS), F32),
        pltpu.VMEM((rows, dff), BF16),
        pltpu.VMEM((rows, dple), BF16),
    ]


def _layer_param(arr, layer):
    if arr.ndim == 2:
        return pl.BlockSpec(arr.shape, lambda i: (0, 0), pipeline_mode=pl.Buffered(1))
    _, r, c = arr.shape
    return pl.BlockSpec((None, r, c), lambda i: (layer, 0, 0), pipeline_mode=pl.Buffered(1))


def _layer_weight(arr, layer):
    _, r, c = arr.shape
    assert r % (WEIGHT_STEPS * BF16_ROWS) == 0
    last = WEIGHT_STEPS - 1
    return pl.BlockSpec((None, r // WEIGHT_STEPS, c), lambda i: (layer, jnp.minimum(i, last), 0))


def _tile_index(i):
    return jnp.maximum(i - WEIGHT_STEPS, 0)


def _row_tiles(cols):
    return pl.BlockSpec((ROW_TILE, cols), lambda i: (_tile_index(i), 0))


def _compiler_params():
    return pltpu.CompilerParams(dimension_semantics=("arbitrary",),
                                vmem_limit_bytes=VMEM_LIMIT_BYTES)


def _mixer_call(x, seq, layer, g, w_in, b_gate, caw, cab, lng, lnb, wa, cbw, wb, wo):
    n, dm = x.shape
    ka, dc = caw.shape[1] // SUBLANES, caw.shape[2]
    kb, ds = cbw.shape[1:]
    assert ds % MXU_COLS == 0 and dm % MXU_COLS == 0 and dc % LANES == 0
    n_mid = (w_in.shape[2] - 2 * dc) // MXU_COLS
    spec = {id(w): _layer_weight(w, layer) for w in (w_in, wa, wb, wo)}
    operands = (g, w_in, b_gate, caw, cab, lng, lnb, wa, cbw, wb, wo)
    return pl.pallas_call(
        functools.partial(_mixer_kernel, tiles_per_seq=seq // ROW_TILE, layer=layer),
        grid=(WEIGHT_STEPS + n // ROW_TILE,),
        in_specs=[_row_tiles(dm)] + [spec.get(id(a)) or _layer_param(a, layer) for a in operands],
        out_specs=_row_tiles(dm),
        out_shape=jax.ShapeDtypeStruct((n, dm), x.dtype),
        scratch_shapes=_mixer_scratch(ROW_TILE, dm, ka, dc, kb, ds, n_mid),
        compiler_params=_compiler_params(),
        name="token_mixer",
    )(x, *operands)


def _ffn_call(x, p, seq, layer, gf, wup, cfw, cfb, wdn, gp, wple, wpg, gfin, final_norm):
    n, dm = x.shape
    kf, dff = cfw.shape[1:]
    dple = p.shape[2]
    spec = {id(w): _layer_weight(w, layer) for w in (wup, wdn, wple, wpg)}
    operands = (gf, wup, cfw, cfb, wdn, gp, wple, wpg)
    p_spec = pl.BlockSpec((None, ROW_TILE, dple), lambda i: (layer, _tile_index(i), 0))
    gfin_spec = pl.BlockSpec(gfin.shape, lambda i: (0, 0), pipeline_mode=pl.Buffered(1))
    return pl.pallas_call(
        functools.partial(_ffn_kernel, tiles_per_seq=seq // ROW_TILE, layer=layer,
                          final_norm=final_norm),
        grid=(WEIGHT_STEPS + n // ROW_TILE,),
        in_specs=([_row_tiles(dm), p_spec]
                  + [spec.get(id(a)) or _layer_param(a, layer) for a in operands] + [gfin_spec]),
        out_specs=_row_tiles(dm),
        out_shape=jax.ShapeDtypeStruct((n, dm), x.dtype),
        scratch_shapes=_ffn_scratch(ROW_TILE, dm, kf, dff, dple),
        compiler_params=_compiler_params(),
        name="channel_mixer",
    )(x, p, *operands, gfin)


def kernel(x, p, g_mix, w_in, b_gate, conv_a_w, conv_a_b, ln_a_g, ln_a_b, w_a_out, conv_b_w, w_b_out, w_o, g_ffn, w_up, conv_f_w, conv_f_b, w_down, g_ple, w_ple, w_ple_gate, g_final):
    batch, seq, dm = x.shape
    depth = p.shape[0]
    assert seq % ROW_TILE == 0
    caw = jnp.repeat(conv_a_w, SUBLANES, axis=1)
    xf = x.reshape(batch * seq, dm)
    pf = p.reshape(depth, batch * seq, -1)
    for i in range(depth):
        xf = _mixer_call(xf, seq, i, g_mix, w_in, b_gate, caw, conv_a_b, ln_a_g, ln_a_b,
                         w_a_out, conv_b_w, w_b_out, w_o)
        xf = _ffn_call(xf, pf, seq, i, g_ffn, w_up, conv_f_w, conv_f_b, w_down,
                       g_ple, w_ple, w_ple_gate, g_final.reshape(1, -1),
                       final_norm=(i == depth - 1))
    return xf.reshape(batch, seq, dm)
```

```python
import functools

import jax
import jax.numpy as jnp
from jax import lax
from jax.experimental import pallas as pl
from jax.experimental.pallas import tpu as pltpu

EPS = 1e-6
SUBLANES = 8
BF16_ROWS = 16
LANES = 128
ROW_TILE = 512
WEIGHT_STEPS = 8
NORM_ROWS = 16
CONV_ROWS = 64
MXU_COLS = 512
FFN_COLS = 256
MID_PER_BLOCK = (3, 1, 2, 1)
SHORT_CONV_BLOCK = 0
VMEM_LIMIT_BYTES = 58 * 1024 * 1024

F32 = jnp.float32
BF16 = jnp.bfloat16


def _round_up(n, m):
    return (n + m - 1) // m * m


def _dot(a, b):
    return jnp.dot(a, b, preferred_element_type=F32)


def _rmsnorm(x, g):
    y = x * lax.rsqrt(jnp.mean(x * x, axis=-1, keepdims=True) + EPS)
    return y * g


def _rmsnorm_rows(src_ref, g_ref, dst_ref):
    g = g_ref[...]
    for r0 in range(0, src_ref.shape[0], NORM_ROWS):
        dst_ref[r0:r0 + NORM_ROWS, :] = _rmsnorm(src_ref[r0:r0 + NORM_ROWS, :], g).astype(dst_ref.dtype)


def _cast_rows(step, src_ref, dst_ref, src_cols=None, dst_col0=0):
    chunk = src_ref.shape[0]
    assert chunk % BF16_ROWS == 0
    rows = pl.ds(pl.multiple_of(step * chunk, chunk), chunk)
    c0, c1 = src_cols or (0, src_ref.shape[1])
    for c in range(c0, c1, MXU_COLS):
        w = min(MXU_COLS, c1 - c)
        dst_ref[rows, dst_col0 + c - c0:dst_col0 + c - c0 + w] = src_ref[:, c:c + w].astype(BF16)


class _Taps:
    def __init__(self, k_taps, cur_in_copies):
        self.halo = _round_up(k_taps - 1, SUBLANES)
        self.qr = [divmod(self.halo - (k_taps - 1) + k, SUBLANES) for k in range(k_taps)]
        self.cur_in_copies = cur_in_copies
        self.residues = sorted({r for q, r in self.qr if cur_in_copies or not self._is_cur(q, r)})

    def _is_cur(self, q, r):
        return (q * SUBLANES, r) == (self.halo, 0)

    def scratch(self, rows, cols):
        return pltpu.VMEM((len(self.residues), self.halo + rows, cols), F32)

    def carry(self, buf, rows, first):
        @pl.when(first)
        def _():
            buf[...] = jnp.zeros(buf.shape, buf.dtype)

        @pl.when(jnp.logical_not(first))
        def _():
            buf[:, 0:self.halo, :] = buf[:, rows:rows + self.halo, :]

    def store(self, buf, value, c0=0):
        rows, cols = value.shape
        for slot, r in enumerate(self.residues):
            buf[slot, self.halo - r:self.halo - r + rows, c0:c0 + cols] = value

    def tap(self, buf, k, r0, nrows, c0, c1, cur=None):
        q, r = self.qr[k]
        if self._is_cur(q, r) and not self.cur_in_copies:
            return cur
        start = r0 + q * SUBLANES
        return buf[self.residues.index(r), start:start + nrows, c0:c1]


_TAPS_A = functools.partial(_Taps, cur_in_copies=True)
_TAPS_3 = functools.partial(_Taps, cur_in_copies=False)


def _mixer_kernel(x_ref, g_ref, win_ref, bg_ref, caw_ref, cab_ref, lng_ref, lnb_ref,
                  wa_ref, cbw_ref, wb_ref, wo_ref, o_ref,
                  win_s, wa_s, wb_s, wo_s,
                  abuf, cvbuf, h_ref, conv_ref, mid_ref, act_ref, s_ref, m_ref, mb_ref,
                  *, tiles_per_seq, layer):
    g_ref, bg_ref, cab_ref, lng_ref, lnb_ref = (
        r.at[layer:layer + 1] for r in (g_ref, bg_ref, cab_ref, lng_ref, lnb_ref))
    rows, dm = x_ref.shape
    ka, dc = caw_ref.shape[0] // SUBLANES, caw_ref.shape[1]
    kb, ds = cbw_ref.shape
    n_blocks = dc // LANES
    n_mid = mid_ref.shape[0]
    i_b, i_c, i_v = 0, ds // MXU_COLS, 2 * ds // MXU_COLS
    i_ga, i_gb = 3 * ds // MXU_COLS, (3 * ds + dm) // MXU_COLS
    taps_a, taps_b = _TAPS_A(ka), _TAPS_3(kb)
    step = pl.program_id(0)
    mid_of, col_of, col = [], [], 0
    for b in range(n_blocks):
        first_mid = sum(MID_PER_BLOCK[:b])
        mid_of.append(list(range(first_mid, first_mid + MID_PER_BLOCK[b])))
        col_of.append(col)
        col += 2 * LANES + MID_PER_BLOCK[b] * MXU_COLS
    assert sum(MID_PER_BLOCK) == n_mid

    @pl.when(step < WEIGHT_STEPS)
    def _():
        for b in range(n_blocks):
            _cast_rows(step, win_ref, win_s, (b * LANES, (b + 1) * LANES), col_of[b])
            _cast_rows(step, win_ref, win_s, (dc + b * LANES, dc + (b + 1) * LANES),
                       col_of[b] + LANES)
            for j, i in enumerate(mid_of[b]):
                _cast_rows(step, win_ref, win_s, (2 * dc + i * MXU_COLS, 2 * dc + (i + 1) * MXU_COLS),
                           col_of[b] + 2 * LANES + j * MXU_COLS)
        _cast_rows(step, wa_ref, wa_s)
        _cast_rows(step, wb_ref, wb_s)
        _cast_rows(step, wo_ref, wo_s)

    @pl.when(step >= WEIGHT_STEPS)
    def _():
        first = ((step - WEIGHT_STEPS) % tiles_per_seq) == 0
        taps_a.carry(abuf, rows, first)
        taps_b.carry(cvbuf, rows, first)

        _rmsnorm_rows(x_ref, g_ref, h_ref)

        n_groups = CONV_ROWS // SUBLANES

        def glu_block(b):
            width = 2 * LANES + len(mid_of[b]) * MXU_COLS
            z = _dot(h_ref[...], win_s[:, col_of[b]:col_of[b] + width])
            taps_a.store(abuf, z[:, :LANES] * jax.nn.sigmoid(z[:, LANES:2 * LANES]), b * LANES)
            for j, i in enumerate(mid_of[b]):
                mid_ref[i] = z[:, 2 * LANES + j * MXU_COLS:2 * LANES + (j + 1) * MXU_COLS]

        def conv_block(b):
            lanes = slice(b * LANES, (b + 1) * LANES)
            for r0 in range(0, rows, CONV_ROWS):
                w = [caw_ref[k * SUBLANES:(k + 1) * SUBLANES, lanes] for k in range(ka)]
                accs = [jnp.broadcast_to(cab_ref[:, lanes], (SUBLANES, LANES))] * n_groups
                for slot, r in enumerate(taps_a.residues):
                    qk = [(q, k) for k, (q, rr) in enumerate(taps_a.qr) if rr == r]
                    for m in range(min(q for q, _ in qk), n_groups + max(q for q, _ in qk)):
                        win = abuf[slot, r0 + m * SUBLANES:r0 + (m + 1) * SUBLANES, lanes]
                        for q, k in qk:
                            if 0 <= m - q < n_groups:
                                accs[m - q] = accs[m - q] + w[k] * win
                conv_ref[r0:r0 + CONV_ROWS, lanes] = jnp.concatenate(accs, axis=0)

        def layernorm_swish():
            lng, lnb = lng_ref[...], lnb_ref[...]
            for r0 in range(0, rows, NORM_ROWS):
                acc = conv_ref[r0:r0 + NORM_ROWS, :]
                mu = jnp.mean(acc, axis=-1, keepdims=True)
                cen = acc - mu
                var = jnp.mean(cen * cen, axis=-1, keepdims=True)
                y = cen * lax.rsqrt(var + EPS) * lng + lnb
                act_ref[r0:r0 + NORM_ROWS, :] = (y * jax.nn.sigmoid(y)).astype(BF16)

        def short_conv():
            for j in range(ds // MXU_COLS):
                c0, c1 = j * MXU_COLS, (j + 1) * MXU_COLS
                cv = mid_ref[i_c + j] * mid_ref[i_v + j]
                taps_b.store(cvbuf, cv, c0)
                conv = cbw_ref[0:1, c0:c1] * taps_b.tap(cvbuf, 0, 0, rows, c0, c1, cv)
                for k in range(1, kb):
                    conv = conv + cbw_ref[k:k + 1, c0:c1] * taps_b.tap(cvbuf, k, 0, rows, c0, c1, cv)
                s_ref[:, c0:c1] = (mid_ref[i_b + j] * conv).astype(BF16)

        def project_b():
            for j in range(dm // MXU_COLS):
                c0, c1 = j * MXU_COLS, (j + 1) * MXU_COLS
                m_ref[:, c0:c1] = _dot(s_ref[...], wb_s[:, c0:c1])

        for b in range(n_blocks):
            glu_block(b)
            if b == SHORT_CONV_BLOCK:
                short_conv()
            if b == n_blocks - 1:
                project_b()
            conv_block(b)
        layernorm_swish()

        for j in range(dm // MXU_COLS):
            c0, c1 = j * MXU_COLS, (j + 1) * MXU_COLS
            g_a = jax.nn.sigmoid(mid_ref[i_ga + j] + bg_ref[:, c0:c1])
            g_b = jax.nn.sigmoid(mid_ref[i_gb + j] + bg_ref[:, dm + c0:dm + c1])
            y_a = _dot(act_ref[...], wa_s[:, c0:c1])
            mb_ref[:, c0:c1] = (g_a * y_a + g_b * m_ref[:, c0:c1]).astype(BF16)
        o_ref[...] = x_ref[...] + _dot(mb_ref[...], wo_s[...])


def _mixer_scratch(rows, dm, ka, dc, kb, ds, n_mid):
    return [
        pltpu.VMEM((dm, 2 * dc + n_mid * MXU_COLS), BF16),
        pltpu.VMEM((dc, dm), BF16),
        pltpu.VMEM((ds, dm), BF16),
        pltpu.VMEM((dm, dm), BF16),
        _TAPS_A(ka).scratch(rows, dc), _TAPS_3(kb).scratch(rows, ds),
        pltpu.VMEM((rows, dm), BF16),
        pltpu.VMEM((rows, dc), F32),
        pltpu.VMEM((n_mid, rows, MXU_COLS), F32),
        pltpu.VMEM((rows, dc), BF16),
        pltpu.VMEM((rows, ds), BF16),
        pltpu.VMEM((rows, dm), F32),
        pltpu.VMEM((rows, dm), BF16),
    ]


def _ffn_kernel(x_ref, p_ref, gf_ref, wup_ref, cfw_ref, cfb_ref, wdn_ref, gp_ref, wple_ref,
                wpg_ref, gfin_ref, o_ref,
                wup_s, wdn_s, wple_s, wpg_s, gbuf, h_ref, t0_ref, fbuf, pb_ref,
                *, tiles_per_seq, layer, final_norm):
    gf_ref, cfb_ref, gp_ref = (r.at[layer:layer + 1] for r in (gf_ref, cfb_ref, gp_ref))
    rows, dm = x_ref.shape
    kf, dff = cfw_ref.shape
    taps = _TAPS_3(kf)
    step = pl.program_id(0)

    @pl.when(step < WEIGHT_STEPS)
    def _():
        for c0 in range(0, dff, FFN_COLS):
            _cast_rows(step, wup_ref, wup_s, (c0, c0 + FFN_COLS), 2 * c0)
            _cast_rows(step, wup_ref, wup_s, (dff + c0, dff + c0 + FFN_COLS), 2 * c0 + FFN_COLS)
        _cast_rows(step, wdn_ref, wdn_s)
        _cast_rows(step, wple_ref, wple_s)
        _cast_rows(step, wpg_ref, wpg_s)

    @pl.when(step >= WEIGHT_STEPS)
    def _():
        first = ((step - WEIGHT_STEPS) % tiles_per_seq) == 0
        taps.carry(gbuf, rows, first)

        _rmsnorm_rows(x_ref, gf_ref, h_ref)
        for c0 in range(0, dff, FFN_COLS):
            c1 = min(c0 + FFN_COLS, dff)
            z = _dot(h_ref[...], wup_s[:, 2 * c0:2 * c1])
            gate, val = z[:, :FFN_COLS], z[:, FFN_COLS:]
            t0_ref[:, 0:c1 - c0] = gate
            taps.store(gbuf, gate, c0)
            conv = jnp.broadcast_to(cfb_ref[:, c0:c1], (rows, c1 - c0))
            for k in range(kf):
                conv = conv + cfw_ref[k:k + 1, c0:c1] * taps.tap(gbuf, k, 0, rows, c0, c1,
                                                                t0_ref[:, 0:c1 - c0])
            fbuf[:, c0:c1] = (jax.nn.gelu(conv, approximate=True) * val).astype(BF16)
        o_ref[...] = x_ref[...] + _dot(fbuf[...], wdn_s[...])

        _rmsnorm_rows(o_ref, gp_ref, h_ref)
        pb_ref[...] = p_ref[...].astype(BF16)
        for c0 in range(0, dm, FFN_COLS):
            c1 = c0 + FFN_COLS
            gate = jax.nn.sigmoid(_dot(h_ref[...], wpg_s[:, c0:c1]))
            o_ref[:, c0:c1] = o_ref[:, c0:c1] + gate * _dot(pb_ref[...], wple_s[:, c0:c1])
        if final_norm:
            _rmsnorm_rows(o_ref, gfin_ref, o_ref)


def _ffn_scratch(rows, dm, kf, dff, dple):
    return [
        pltpu.VMEM((dm, 2 * dff), BF16),
        pltpu.VMEM((dff, dm), BF16),
        pltpu.VMEM((dple, dm), BF16),
        pltpu.VMEM((dm, dm), BF16),
        _TAPS_3(kf).scratch(rows, dff),
        pltpu.VMEM((rows, dm), BF16),
        pltpu.VMEM((rows, FFN_COLS), F32),
        pltpu.VMEM((rows, dff), BF16),
        pltpu.VMEM((rows, dple), BF16),
    ]


def _layer_param(arr, layer):
    if arr.ndim == 2:
        return pl.BlockSpec(arr.shape, lambda i: (0, 0), pipeline_mode=pl.Buffered(1))
    _, r, c = arr.shape
    return pl.BlockSpec((None, r, c), lambda i: (layer, 0, 0), pipeline_mode=pl.Buffered(1))


def _layer_weight(arr, layer):
    _, r, c = arr.shape
    assert r % (WEIGHT_STEPS * BF16_ROWS) == 0
    last = WEIGHT_STEPS - 1
    return pl.BlockSpec((None, r // WEIGHT_STEPS, c), lambda i: (layer, jnp.minimum(i, last), 0))


def _tile_index(i):
    return jnp.maximum(i - WEIGHT_STEPS, 0)


def _row_tiles(cols):
    return pl.BlockSpec((ROW_TILE, cols), lambda i: (_tile_index(i), 0))


def _compiler_params():
    return pltpu.CompilerParams(dimension_semantics=("arbitrary",),
                                vmem_limit_bytes=VMEM_LIMIT_BYTES)


def _mixer_call(x, seq, layer, g, w_in, b_gate, caw, cab, lng, lnb, wa, cbw, wb, wo):
    n, dm = x.shape
    ka, dc = caw.shape[1] // SUBLANES, caw.shape[2]
    kb, ds = cbw.shape[1:]
    assert ds % MXU_COLS == 0 and dm % MXU_COLS == 0 and dc % LANES == 0
    n_mid = (w_in.shape[2] - 2 * dc) // MXU_COLS
    spec = {id(w): _layer_weight(w, layer) for w in (w_in, wa, wb, wo)}
    operands = (g, w_in, b_gate, caw, cab, lng, lnb, wa, cbw, wb, wo)
    return pl.pallas_call(
        functools.partial(_mixer_kernel, tiles_per_seq=seq // ROW_TILE, layer=layer),
        grid=(WEIGHT_STEPS + n // ROW_TILE,),
        in_specs=[_row_tiles(dm)] + [spec.get(id(a)) or _layer_param(a, layer) for a in operands],
        out_specs=_row_tiles(dm),
        out_shape=jax.ShapeDtypeStruct((n, dm), x.dtype),
        scratch_shapes=_mixer_scratch(ROW_TILE, dm, ka, dc, kb, ds, n_mid),
        compiler_params=_compiler_params(),
        name="token_mixer",
    )(x, *operands)


def _ffn_call(x, p, seq, layer, gf, wup, cfw, cfb, wdn, gp, wple, wpg, gfin, final_norm):
    n, dm = x.shape
    kf, dff = cfw.shape[1:]
    dple = p.shape[2]
    assert dff % FFN_COLS == 0 and dm % FFN_COLS == 0
    spec = {id(w): _layer_weight(w, layer) for w in (wup, wdn, wple, wpg)}
    operands = (gf, wup, cfw, cfb, wdn, gp, wple, wpg)
    p_spec = pl.BlockSpec((None, ROW_TILE, dple), lambda i: (layer, _tile_index(i), 0))
    gfin_spec = pl.BlockSpec(gfin.shape, lambda i: (0, 0), pipeline_mode=pl.Buffered(1))
    return pl.pallas_call(
        functools.partial(_ffn_kernel, tiles_per_seq=seq // ROW_TILE, layer=layer,
                          final_norm=final_norm),
        grid=(WEIGHT_STEPS + n // ROW_TILE,),
        in_specs=([_row_tiles(dm), p_spec]
                  + [spec.get(id(a)) or _layer_param(a, layer) for a in operands] + [gfin_spec]),
        out_specs=_row_tiles(dm),
        out_shape=jax.ShapeDtypeStruct((n, dm), x.dtype),
        scratch_shapes=_ffn_scratch(ROW_TILE, dm, kf, dff, dple),
        compiler_params=_compiler_params(),
        name="channel_mixer",
    )(x, p, *operands, gfin)


def kernel(x, p, g_mix, w_in, b_gate, conv_a_w, conv_a_b, ln_a_g, ln_a_b, w_a_out, conv_b_w, w_b_out, w_o, g_ffn, w_up, conv_f_w, conv_f_b, w_down, g_ple, w_ple, w_ple_gate, g_final):
    batch, seq, dm = x.shape
    depth = p.shape[0]
    assert seq % ROW_TILE == 0
    caw = jnp.repeat(conv_a_w, SUBLANES, axis=1)
    xf = x.reshape(batch * seq, dm)
    pf = p.reshape(depth, batch * seq, -1)
    for i in range(depth):
        xf = _mixer_call(xf, seq, i, g_mix, w_in, b_gate, caw, conv_a_b, ln_a_g, ln_a_b,
                         w_a_out, conv_b_w, w_b_out, w_o)
        xf = _ffn_call(xf, pf, seq, i, g_ffn, w_up, conv_f_w, conv_f_b, w_down,
                       g_ple, w_ple, w_ple_gate, g_final.reshape(1, -1),
                       final_norm=(i == depth - 1))
    return xf.reshape(batch, seq, dm)
```
